```python
import math
import jax, jax.numpy as jnp
from jax import lax
import numpy as np

D_MODEL = 2048
BATCH = 8
SEQ = 2048
DEPTH = 1
DEC_BATCH = 2
DEC_SEQ = 8192
PAST_LEN = 128

ATT_HEADS = 16
ATT_HEAD_DIM = 64
ATT_WIDTH = ATT_HEADS * ATT_HEAD_DIM
DILATED_PAIRS = ((128, 1), (512, 4), (2048, 16))
N_BUCKETS = 32
MAX_DISTANCE = 1024
MLSTM_HEADS = 4
MLSTM_HEAD_DIM = 256
MLSTM_WIDTH = MLSTM_HEADS * MLSTM_HEAD_DIM
MLSTM_CHUNK = 64
CONV_K = 3
N_GATES = 4 * MLSTM_HEADS
IN_COLS = 3 * ATT_WIDTH + 4 * MLSTM_WIDTH + N_GATES
D_FF = 5632
PLE_DIM = 256
EPS = 1e-6
NEG = -1e30

kernel_name = 'hybrid_dilated_attn_mlstm_macaron_encoder'


def rms_norm(x, w):
    xf = x.astype(jnp.float32)
    y = xf * lax.rsqrt(jnp.mean(xf * xf, axis=-1, keepdims=True) + EPS)
    return (y * w.astype(jnp.float32)).astype(x.dtype)


def swiglu(x, w_in, w_out):
    g, u = jnp.split(x @ w_in, 2, axis=-1)
    return (jax.nn.silu(g) * u) @ w_out


def t5_bucket(rel):
    nb = N_BUCKETS // 2
    max_exact = nb // 2
    n = np.abs(rel)
    large = max_exact + (np.log(np.maximum(n, 1) / max_exact) / math.log(MAX_DISTANCE / max_exact) * (nb - max_exact)).astype(np.int32)
    large = np.minimum(large, nb - 1)
    return np.where(rel > 0, nb, 0) + np.where(n < max_exact, n, large)


def dilated_branch(q, k, v, rel_table, window, dilation):
    B, S, H, E = q.shape
    R = window // (2 * dilation)
    L = S // dilation
    nb = -(-L // R)
    Lp = nb * R

    def split(t):
        t = t.reshape(B, L, dilation, H, E).transpose(0, 2, 1, 3, 4)
        return jnp.pad(t, ((0, 0), (0, 0), (0, Lp - L), (0, 0), (0, 0)))

    def windows(t):
        t = jnp.pad(split(t), ((0, 0), (0, 0), (R, R), (0, 0), (0, 0))).reshape(B, dilation, nb + 2, R, H, E)
        return jnp.concatenate([t[:, :, :-2], t[:, :, 1:-1], t[:, :, 2:]], axis=3).astype(jnp.float32)

    qs = split(q).reshape(B, dilation, nb, R, H, E).astype(jnp.float32)
    kw, vw = windows(k), windows(v)
    off = np.arange(3 * R)[None, :] - R - np.arange(R)[:, None]
    band = np.abs(off) <= R
    kpos = (np.arange(nb)[:, None] - 1) * R + np.arange(3 * R)[None, :]
    mask = band[None] & ((kpos >= 0) & (kpos < L))[:, None, :]
    bias = jnp.transpose(rel_table[t5_bucket(off * dilation)], (2, 0, 1)).astype(jnp.float32)
    s = jnp.einsum('bdnqhe,bdnkhe->bdnhqk', qs, kw) * (E ** -0.5) + bias
    s = jnp.where(mask[:, None], s, NEG)
    mx = jnp.max(s, axis=-1)
    e = jnp.exp(s - mx[..., None])
    den = jnp.sum(e, axis=-1)
    num = jnp.einsum('bdnhqk,bdnkhe->bdnqhe', e, vw)

    def merge(t):
        tail = t.shape[4:]
        t = t.reshape(B, dilation, Lp, *tail)[:, :, :L]
        return jnp.swapaxes(t, 1, 2).reshape(B, S, *tail)

    return merge(num), merge(jnp.swapaxes(mx, -1, -2)), merge(jnp.swapaxes(den, -1, -2))


def dilated_attention(q, k, v, rel_table):
    B, S, _ = q.shape
    sh = lambda t: t.reshape(B, S, ATT_HEADS, ATT_HEAD_DIM)
    parts = [dilated_branch(sh(q), sh(k), sh(v), rel_table, w, d) for (w, d) in DILATED_PAIRS]
    m_all = parts[0][1]
    for _, m_b, _ in parts[1:]:
        m_all = jnp.maximum(m_all, m_b)
    num = 0.0
    den = 0.0
    for n_b, m_b, d_b in parts:
        scale = jnp.exp(m_b - m_all)
        num = num + n_b * scale[..., None]
        den = den + d_b * scale
    out = num / den[..., None]
    return out.reshape(B, S, ATT_WIDTH).astype(q.dtype)


def mlstm_scan(q, k, v, log_i, log_f):
    B, H, S, E = q.shape
    C = MLSTM_CHUNK
    nc = S // C

    def chunks(t):
        return jnp.moveaxis(t.reshape(B, H, nc, C, *t.shape[3:]), 2, 0)

    lower_tri = jnp.tril(jnp.ones((C, C), dtype=bool))

    def step(carry, xs):
        c_mat, n_vec, m = carry
        qc, kc, vc, li, lf = xs
        b = jnp.cumsum(lf, axis=-1)
        g = b[..., -1]
        d_mat = jnp.where(lower_tri, b[..., :, None] - b[..., None, :] + li[..., None, :], -jnp.inf)
        inter = b + m[..., None]
        m_t = jnp.maximum(inter, jnp.max(d_mat, axis=-1))
        w_intra = jnp.exp(d_mat - m_t[..., None]) * jnp.einsum('bhtk,bhsk->bhts', qc, kc)
        w_inter = jnp.exp(inter - m_t)
        num = jnp.einsum('bhts,bhsv->bhtv', w_intra, vc) + w_inter[..., None] * jnp.einsum('bhvk,bhtk->bhtv', c_mat, qc)
        den = jnp.sum(w_intra, axis=-1) + w_inter * jnp.einsum('bhk,bhtk->bht', n_vec, qc)
        h = num / jnp.maximum(jnp.abs(den), jnp.exp(-m_t))[..., None]
        a = g[..., None] - b + li
        m_new = jnp.maximum(g + m, jnp.max(a, axis=-1))
        w_a = jnp.exp(a - m_new[..., None])
        decay = jnp.exp(g + m - m_new)
        c_new = decay[..., None, None] * c_mat + jnp.einsum('bhsv,bhsk->bhvk', vc * w_a[..., None], kc)
        n_new = decay[..., None] * n_vec + jnp.einsum('bhs,bhsk->bhk', w_a, kc)
        return (c_new, n_new, m_new), h

    init = (jnp.zeros((B, H, E, E), jnp.float32), jnp.zeros((B, H, E), jnp.float32),
            jnp.full((B, H), NEG, jnp.float32))
    _, h = lax.scan(step, init, (chunks(q), chunks(k), chunks(v), chunks(log_i), chunks(log_f)))
    return jnp.moveaxis(h, 0, 2).reshape(B, H, S, E)


def mlstm_mixer(q, k, v, o_pre, gates, norm_w):
    B, S, _ = q.shape
    heads = lambda t: t.reshape(B, S, MLSTM_HEADS, MLSTM_HEAD_DIM).transpose(0, 2, 1, 3).astype(jnp.float32)
    qh, kh, vh = heads(q), heads(k) * (MLSTM_HEAD_DIM ** -0.5), heads(v)
    i_f, f_f, i_b, f_b = gates.astype(jnp.float32).reshape(B, S, 4, MLSTM_HEADS).transpose(2, 0, 3, 1)
    h_fwd = mlstm_scan(qh, kh, vh, i_f, jax.nn.log_sigmoid(f_f))
    flip = lambda t: jnp.flip(t, axis=2)
    h_bwd = flip(mlstm_scan(flip(qh), flip(kh), flip(vh), flip(i_b), jax.nn.log_sigmoid(flip(f_b))))
    h = h_fwd + h_bwd
    mu = jnp.mean(h, axis=-1, keepdims=True)
    var = jnp.mean(jnp.square(h - mu), axis=-1, keepdims=True)
    h = ((h - mu) * lax.rsqrt(var + EPS)).transpose(0, 2, 1, 3).reshape(B, S, MLSTM_WIDTH)
    h = h * norm_w.astype(jnp.float32) * jax.nn.sigmoid(o_pre.astype(jnp.float32))
    return h.astype(q.dtype)


def centred_dwconv(x, w, b):
    K = w.shape[0]
    y = lax.conv_general_dilated(x, w[:, None, :].astype(x.dtype), window_strides=(1,),
                                 padding=[(K // 2, K // 2)], dimension_numbers=('NWC', 'WIO', 'NWC'),
                                 feature_group_count=x.shape[-1])
    return y + b.astype(x.dtype)


def token_mixing(h, rel_table, w_in, b_gates, conv_w, conv_b, attn_out_norm, mlstm_out_norm, w_out):
    A, M = ATT_WIDTH, MLSTM_WIDTH
    z = h @ w_in
    aq, ak, av = z[..., :A], z[..., A:2 * A], z[..., 2 * A:3 * A]
    qk_m = jax.nn.silu(centred_dwconv(z[..., 3 * A:3 * A + 2 * M], conv_w, conv_b))
    mq, mk = qk_m[..., :M], qk_m[..., M:]
    mv = z[..., 3 * A + 2 * M:3 * A + 3 * M]
    mo = z[..., 3 * A + 3 * M:3 * A + 4 * M]
    gates = z[..., 3 * A + 4 * M:] + b_gates.astype(z.dtype)
    y_att = rms_norm(dilated_attention(aq, ak, av, rel_table), attn_out_norm)
    y_mem = mlstm_mixer(mq, mk, mv, mo, gates, mlstm_out_norm)
    return jnp.concatenate([y_att, y_mem], axis=-1) @ w_out


def encoder_trunk(x, pe, rel_table, ffn1_norm, ffn1_w_in, ffn1_w_out, mix_norm, w_in, b_gates,
                  conv_w, conv_b, attn_out_norm, mlstm_out_norm, w_out, ffn2_norm, ffn2_w_in,
                  ffn2_w_out, ple_norm, ple_w_gate, ple_w_proj, final_norm):
    for i in range(DEPTH):
        x = x + 0.5 * swiglu(rms_norm(x, ffn1_norm[i]), ffn1_w_in[i], ffn1_w_out[i])
        x = x + token_mixing(rms_norm(x, mix_norm[i]), rel_table, w_in[i], b_gates[i], conv_w[i], conv_b[i],
                             attn_out_norm[i], mlstm_out_norm[i], w_out[i])
        x = x + 0.5 * swiglu(rms_norm(x, ffn2_norm[i]), ffn2_w_in[i], ffn2_w_out[i])
        gate = jax.nn.sigmoid(rms_norm(x, ple_norm[i]) @ ple_w_gate[i])
        x = x + gate * (pe[i] @ ple_w_proj[i])
    return rms_norm(x, final_norm)


def setup_inputs(seed: int = 0) -> dict:
    key = jax.random.key(seed)
    ks = jax.random.split(key, 24)
    f32 = jnp.float32
    nrm = lambda k, shape, s: jax.random.normal(k, shape, f32) * s
    gain = lambda k, shape: 1.0 + 0.01 * jax.random.normal(k, shape, f32)
    i_bias = nrm(ks[6], (DEPTH, MLSTM_HEADS), 0.1)
    f_bias = jnp.linspace(3.0, 6.0, MLSTM_HEADS, dtype=f32)[None] + nrm(ks[7], (DEPTH, MLSTM_HEADS), 0.01)
    i_bias_b = nrm(ks[8], (DEPTH, MLSTM_HEADS), 0.1)
    f_bias_b = jnp.linspace(3.0, 6.0, MLSTM_HEADS, dtype=f32)[None] + nrm(ks[9], (DEPTH, MLSTM_HEADS), 0.01)
    return {
        'x_prompt': nrm(ks[0], (BATCH, SEQ, D_MODEL), 1.0),
        'x_sample': nrm(ks[1], (DEC_BATCH, DEC_SEQ, D_MODEL), 1.0),
        'p_prompt': nrm(ks[2], (DEPTH, BATCH, SEQ, PLE_DIM), 1.0),
        'p_sample': nrm(ks[3], (DEPTH, DEC_BATCH, DEC_SEQ, PLE_DIM), 1.0),
        'rel_table': nrm(ks[4], (N_BUCKETS, ATT_HEADS), 0.2),
        'ffn1_norm': gain(ks[5], (DEPTH, D_MODEL)),
        'ffn1_w_in': nrm(ks[10], (DEPTH, D_MODEL, 2 * D_FF), D_MODEL ** -0.5),
        'ffn1_w_out': nrm(ks[11], (DEPTH, D_FF, D_MODEL), D_FF ** -0.5),
        'mix_norm': gain(ks[12], (DEPTH, D_MODEL)),
        'w_in': nrm(ks[13], (DEPTH, D_MODEL, IN_COLS), D_MODEL ** -0.5),
        'b_gates': jnp.concatenate([i_bias, f_bias, i_bias_b, f_bias_b], axis=-1),
        'conv_w': nrm(ks[14], (DEPTH, CONV_K, 2 * MLSTM_WIDTH), CONV_K ** -0.5),
        'conv_b': nrm(ks[15], (DEPTH, 2 * MLSTM_WIDTH), 0.01),
        'attn_out_norm': gain(ks[16], (DEPTH, ATT_WIDTH)),
        'mlstm_out_norm': gain(ks[17], (DEPTH, MLSTM_WIDTH)),
        'w_out': nrm(ks[18], (DEPTH, D_MODEL, D_MODEL), D_MODEL ** -0.5),
        'ffn2_norm': gain(ks[19], (DEPTH, D_MODEL)),
        'ffn2_w_in': nrm(ks[20], (DEPTH, D_MODEL, 2 * D_FF), D_MODEL ** -0.5),
        'ffn2_w_out': nrm(ks[21], (DEPTH, D_FF, D_MODEL), D_FF ** -0.5),
        'ple_norm': gain(ks[22], (DEPTH, D_MODEL)),
        'ple_w_gate': nrm(ks[23], (DEPTH, D_MODEL, D_MODEL), D_MODEL ** -0.5),
        'ple_w_proj': nrm(jax.random.fold_in(key, 101), (DEPTH, PLE_DIM, D_MODEL), PLE_DIM ** -0.5),
        'final_norm': gain(jax.random.fold_in(key, 102), (D_MODEL,)),
    }


def reference(x_prompt, x_sample, p_prompt, p_sample, rel_table, ffn1_norm, ffn1_w_in, ffn1_w_out,
              mix_norm, w_in, b_gates, conv_w, conv_b, attn_out_norm, mlstm_out_norm, w_out,
              ffn2_norm, ffn2_w_in, ffn2_w_out, ple_norm, ple_w_gate, ple_w_proj, final_norm):
    y_prompt = encoder_trunk(x_prompt, p_prompt, rel_table, ffn1_norm, ffn1_w_in, ffn1_w_out, mix_norm, w_in,
                             b_gates, conv_w, conv_b, attn_out_norm, mlstm_out_norm, w_out, ffn2_norm,
                             ffn2_w_in, ffn2_w_out, ple_norm, ple_w_gate, ple_w_proj, final_norm)
    y_sample = encoder_trunk(x_sample, p_sample, rel_table, ffn1_norm, ffn1_w_in, ffn1_w_out, mix_norm, w_in,
                             b_gates, conv_w, conv_b, attn_out_norm, mlstm_out_norm, w_out, ffn2_norm,
                             ffn2_w_in, ffn2_w_out, ple_norm, ple_w_gate, ple_w_proj, final_norm)
    return (y_prompt, y_sample)
```

```python
import functools
import math

import jax
import jax.numpy as jnp
import numpy as np
from jax import lax
from jax.experimental import pallas as pl
from jax.experimental.pallas import tpu as pltpu

F32 = jnp.float32
BF16 = jnp.bfloat16

D_MODEL = 2048
ATT_HEADS = 16
ATT_HEAD_DIM = 64
ATT_WIDTH = ATT_HEADS * ATT_HEAD_DIM
DILATIONS = (1, 4, 16)
HALF_WINDOW = 64
N_BUCKETS = 32
MAX_DISTANCE = 1024
MLSTM_HEADS = 4
MLSTM_HEAD_DIM = 256
MLSTM_WIDTH = MLSTM_HEADS * MLSTM_HEAD_DIM
N_GATES = 4 * MLSTM_HEADS
D_FF = 5632
PLE_DIM = 256
EPS = 1e-6
NEG = -1e30

LANES = 128
V7X_VMEM_BYTES = 64 * 1024 * 1024
VMEM_CAP = V7X_VMEM_BYTES - 6 * 1024 * 1024
VMEM_SLACK = 8 * 1024 * 1024

Q_BLOCK = 128
K_BLOCK = Q_BLOCK + 2 * HALF_WINDOW
MLSTM_CHUNK = 128

NT_DIMS = (((1,), (1,)), ((), ()))
TN_DIMS = (((0,), (0,)), ((), ()))


def _params(semantics, est_bytes):
    limit = int(min(est_bytes + VMEM_SLACK, VMEM_CAP))
    return pltpu.CompilerParams(dimension_semantics=semantics, vmem_limit_bytes=limit)


def _nbytes(shape, dtype):
    return math.prod(shape) * jnp.dtype(dtype).itemsize


def _rms(x, w):
    ms = jnp.mean(x * x, axis=-1, keepdims=True)
    return x * lax.rsqrt(ms + EPS) * w


def _ffn_kernel(x_ref, nw_ref, wg_ref, wu_ref, wo_ref, o_ref, h_ref):
    f = pl.program_id(1)

    @pl.when(f == 0)
    def _():
        x = x_ref[...]
        h_ref[...] = _rms(x, nw_ref[...]).astype(BF16)
        o_ref[...] = x

    h = h_ref[...]
    g = jnp.dot(h, wg_ref[...], preferred_element_type=F32)
    u = jnp.dot(h, wu_ref[...], preferred_element_type=F32)
    a = (g * jax.nn.sigmoid(g) * u * 0.5).astype(BF16)
    o_ref[...] += jnp.dot(a, wo_ref[...], preferred_element_type=F32)


def _ffn(x, norm_w, w_in, w_out, *, tm=512, tf=512):
    n, d = x.shape
    nf = D_FF // tf
    est = (2 * 2 * _nbytes((tm, d), F32) + _nbytes((tm, d), BF16)
           + 2 * 3 * _nbytes((d, tf), BF16) + 3 * _nbytes((tm, tf), F32))
    return pl.pallas_call(
        _ffn_kernel,
        grid=(n // tm, nf),
        in_specs=[
            pl.BlockSpec((tm, d), lambda i, f: (i, 0)),
            pl.BlockSpec((1, d), lambda i, f: (0, 0)),
            pl.BlockSpec((d, tf), lambda i, f: (0, f)),
            pl.BlockSpec((d, tf), lambda i, f: (0, f + nf)),
            pl.BlockSpec((tf, d), lambda i, f: (f, 0)),
        ],
        out_specs=pl.BlockSpec((tm, d), lambda i, f: (i, 0)),
        out_shape=jax.ShapeDtypeStruct((n, d), F32),
        scratch_shapes=[pltpu.VMEM((tm, d), BF16)],
        compiler_params=_params(("parallel", "arbitrary"), est),
        name="ffn",
    )(x, norm_w.reshape(1, d), w_in, w_in, w_out)


IN_TN = 512
NB_ATT = 3 * ATT_WIDTH // IN_TN
NB_MQK = 2 * MLSTM_WIDTH // IN_TN
NB_MV = MLSTM_WIDTH // IN_TN
NB_MO = MLSTM_WIDTH // IN_TN


def _in_proj_kernel(x_ref, nw_ref, w_ref, wgt_ref, bg_ref,
                    att_ref, mqk_ref, mv_ref, mo_ref, gt_ref, h_ref):
    j = pl.program_id(1)

    @pl.when(j == 0)
    def _():
        h = _rms(x_ref[...], nw_ref[...]).astype(BF16)
        h_ref[...] = h
        gt_ref[...] = lax.dot_general(wgt_ref[...], h, NT_DIMS, preferred_element_type=F32) + bg_ref[...]

    z = jnp.dot(h_ref[...], w_ref[...], preferred_element_type=F32)

    @pl.when(j < NB_ATT)
    def _():
        att_ref[...] = z.astype(BF16)

    @pl.when((j >= NB_ATT) & (j < NB_ATT + NB_MQK))
    def _():
        mqk_ref[...] = z

    @pl.when((j >= NB_ATT + NB_MQK) & (j < NB_ATT + NB_MQK + NB_MV))
    def _():
        mv_ref[...] = z.astype(BF16)

    @pl.when(j >= NB_ATT + NB_MQK + NB_MV)
    def _():
        mo_ref[...] = z


def _in_proj(x, norm_w, w_main, w_gates_t, b_gates, *, tm=1024):
    n, d = x.shape
    tn = IN_TN
    nb = NB_ATT + NB_MQK + NB_MV + NB_MO
    o1, o2, o3 = NB_ATT, NB_ATT + NB_MQK, NB_ATT + NB_MQK + NB_MV

    def col(off, cnt):
        return lambda i, j: (i, jnp.clip(j - off, 0, cnt - 1))

    est = (2 * _nbytes((tm, d), F32) + _nbytes((tm, d), BF16) + 2 * _nbytes((d, tn), BF16)
           + 2 * 2 * (_nbytes((tm, tn), F32) + _nbytes((tm, tn), BF16)) + 2 * _nbytes((tm, tn), F32))
    return pl.pallas_call(
        _in_proj_kernel,
        grid=(n // tm, nb),
        in_specs=[
            pl.BlockSpec((tm, d), lambda i, j: (i, 0)),
            pl.BlockSpec((1, d), lambda i, j: (0, 0)),
            pl.BlockSpec((d, tn), lambda i, j: (0, j)),
            pl.BlockSpec((N_GATES, d), lambda i, j: (0, 0)),
            pl.BlockSpec((N_GATES, 1), lambda i, j: (0, 0)),
        ],
        out_specs=[
            pl.BlockSpec((tm, tn), col(0, NB_ATT)),
            pl.BlockSpec((tm, tn), col(o1, NB_MQK)),
            pl.BlockSpec((tm, tn), col(o2, NB_MV)),
            pl.BlockSpec((tm, tn), col(o3, NB_MO)),
            pl.BlockSpec((N_GATES, tm), lambda i, j: (0, i)),
        ],
        out_shape=[
            jax.ShapeDtypeStruct((n, 3 * ATT_WIDTH), BF16),
            jax.ShapeDtypeStruct((n, 2 * MLSTM_WIDTH), F32),
            jax.ShapeDtypeStruct((n, MLSTM_WIDTH), BF16),
            jax.ShapeDtypeStruct((n, MLSTM_WIDTH), F32),
            jax.ShapeDtypeStruct((N_GATES, n), F32),
        ],
        scratch_shapes=[pltpu.VMEM((tm, d), BF16)],
        compiler_params=_params(("parallel", "arbitrary"), est),
        name="in_proj",
    )(x, norm_w.reshape(1, d), w_main, w_gates_t, b_gates.reshape(N_GATES, 1))


def _rel_bucket(rel):
    nb = N_BUCKETS // 2
    max_exact = nb // 2
    n = np.abs(rel)
    large = max_exact + (np.log(np.maximum(n, 1) / max_exact) / math.log(MAX_DISTANCE / max_exact)
                         * (nb - max_exact)).astype(np.int32)
    large = np.minimum(large, nb - 1)
    return np.where(rel > 0, nb, 0) + np.where(n < max_exact, n, large)


def _branch_bias(rel_table, dilation):
    off = np.arange(K_BLOCK)[None, :] - HALF_WINDOW - np.arange(Q_BLOCK)[:, None]
    band = np.abs(off) <= HALF_WINDOW
    bias = jnp.transpose(rel_table[_rel_bucket(off * dilation)], (2, 0, 1)).astype(F32)
    return jnp.where(band[None], bias, NEG)


def _attn_branch_kernel(q_ref, k_ref, v_ref, bias_ref, o_ref, st_ref, kpad, vpad, *, seq, width):
    pw = pl.program_id(2)
    pairs = width // LANES

    zeros = jnp.zeros((HALF_WINDOW, width), BF16)
    kpad[0:HALF_WINDOW, :] = zeros
    vpad[0:HALF_WINDOW, :] = zeros
    kpad[HALF_WINDOW + seq:2 * HALF_WINDOW + seq, :] = zeros
    vpad[HALF_WINDOW + seq:2 * HALF_WINDOW + seq, :] = zeros
    kpad[HALF_WINDOW:HALF_WINDOW + seq, :] = k_ref[0]
    vpad[HALF_WINDOW:HALF_WINDOW + seq, :] = v_ref[0]

    @pl.when(pw == 0)
    def _():
        st_ref[...] = jnp.zeros_like(st_ref)

    lane = lax.broadcasted_iota(jnp.int32, (1, LANES), 1)
    low = lane < ATT_HEAD_DIM
    col = lax.broadcasted_iota(jnp.int32, (1, K_BLOCK), 1)

    for p in range(pairs):
        sl = slice(p * LANES, (p + 1) * LANES)
        head_pair = pw * pairs + p

        def body(i, carry, sl=sl, p=p, head_pair=head_pair):
            q0 = pl.multiple_of(i * Q_BLOCK, Q_BLOCK)
            qb = q_ref[0, pl.ds(q0, Q_BLOCK), sl]
            kw = kpad[pl.ds(q0, K_BLOCK), sl]
            vw = vpad[pl.ds(q0, K_BLOCK), sl]
            kpos = q0 - HALF_WINDOW + col
            pen = jnp.where((kpos < 0) | (kpos >= seq), NEG, 0.0).astype(F32)
            zero = jnp.zeros_like(qb)
            qs = jnp.concatenate([jnp.where(low, qb, zero), jnp.where(low, zero, qb)], axis=0)
            s = lax.dot_general(qs, kw, NT_DIMS, preferred_element_type=F32)
            s0 = s[:Q_BLOCK] + (bias_ref[2 * p] + pen)
            s1 = s[Q_BLOCK:] + (bias_ref[2 * p + 1] + pen)
            m0 = jnp.max(s0, axis=1, keepdims=True)
            m1 = jnp.max(s1, axis=1, keepdims=True)
            e0 = jnp.exp(s0 - m0)
            e1 = jnp.exp(s1 - m1)
            l0 = jnp.sum(e0, axis=1, keepdims=True)
            l1 = jnp.sum(e1, axis=1, keepdims=True)
            e = jnp.concatenate([e0, e1], axis=0).astype(BF16)
            pv = jnp.dot(e, vw, preferred_element_type=F32)
            o_ref[0, pl.ds(q0, Q_BLOCK), sl] = jnp.where(low, pv[:Q_BLOCK] / l0, pv[Q_BLOCK:] / l1)
            c0 = m0 + jnp.log(l0)
            c1 = m1 + jnp.log(l1)
            prev = st_ref[0, pl.ds(q0, Q_BLOCK), :]
            st_ref[0, pl.ds(q0, Q_BLOCK), :] = jnp.where(
                lane == 2 * head_pair, c0, jnp.where(lane == 2 * head_pair + 1, c1, prev))
            return carry

        lax.fori_loop(0, seq // Q_BLOCK, body, 0)


def _attn_width(seq):
    return int(min(ATT_WIDTH, max(LANES, (1 << 20) // seq // LANES * LANES)))


def _attn_branch(att, bias, dilation):
    b, s, _ = att.shape
    d = dilation
    seq = s // d
    width = _attn_width(seq)
    npw = ATT_WIDTH // width
    cols = 3 * ATT_WIDTH // width
    att_v = att.reshape(b, seq, d * 3 * ATT_WIDTH)
    est = (2 * 3 * _nbytes((seq, width), BF16) + 2 * _nbytes((width // 64, Q_BLOCK, K_BLOCK), F32)
           + 2 * _nbytes((seq, width), F32) + 2 * _nbytes((seq, LANES), F32)
           + 2 * _nbytes((seq + 2 * HALF_WINDOW, width), BF16))
    kern = functools.partial(_attn_branch_kernel, seq=seq, width=width)
    o, st = pl.pallas_call(
        kern,
        grid=(b, d, npw),
        in_specs=[
            pl.BlockSpec((1, seq, width), lambda bi, r, pw: (bi, 0, r * cols + pw)),
            pl.BlockSpec((1, seq, width), lambda bi, r, pw: (bi, 0, r * cols + npw + pw)),
            pl.BlockSpec((1, seq, width), lambda bi, r, pw: (bi, 0, r * cols + 2 * npw + pw)),
            pl.BlockSpec((width // ATT_HEAD_DIM, Q_BLOCK, K_BLOCK), lambda bi, r, pw: (pw, 0, 0)),
        ],
        out_specs=[
            pl.BlockSpec((1, seq, width), lambda bi, r, pw: (bi, 0, r * npw + pw)),
            pl.BlockSpec((1, seq, LANES), lambda bi, r, pw: (bi, 0, r)),
        ],
        out_shape=[
            jax.ShapeDtypeStruct((b, seq, d * ATT_WIDTH), F32),
            jax.ShapeDtypeStruct((b, seq, d * LANES), F32),
        ],
        scratch_shapes=[pltpu.VMEM((seq + 2 * HALF_WINDOW, width), BF16),
                        pltpu.VMEM((seq + 2 * HALF_WINDOW, width), BF16)],
        compiler_params=_params(("parallel", "parallel", "arbitrary"), est),
        name=f"attn_d{d}",
    )(att_v, att_v, att_v, bias)
    return o.reshape(b * s, ATT_WIDTH), st.reshape(b * s, LANES)


def _attn_merge_kernel(o1_ref, o2_ref, o3_ref, s1_ref, s2_ref, s3_ref, ex_ref, nw_ref, y_ref):
    lane = lax.broadcasted_iota(jnp.int32, (1, LANES), 1)
    c1, c2, c3 = s1_ref[...], s2_ref[...], s3_ref[...]
    cm = jnp.maximum(jnp.maximum(c1, c2), c3)
    w1, w2, w3 = jnp.exp(c1 - cm), jnp.exp(c2 - cm), jnp.exp(c3 - cm)
    inv = 1.0 / (w1 + w2 + w3)
    head = lane < ATT_HEADS
    ex = ex_ref[...]

    def spread(w):
        return jnp.dot(jnp.where(head, w * inv, 0.0), ex, preferred_element_type=F32,
                       precision=lax.Precision.HIGHEST)

    out = spread(w1) * o1_ref[...] + spread(w2) * o2_ref[...] + spread(w3) * o3_ref[...]
    y_ref[...] = _rms(out, nw_ref[...]).astype(BF16)


def _attn_merge(outs, stats, norm_w, *, tm=512):
    n = outs[0].shape[0]
    expand = np.zeros((LANES, ATT_WIDTH), np.float32)
    for h in range(ATT_HEADS):
        expand[h, h * ATT_HEAD_DIM:(h + 1) * ATT_HEAD_DIM] = 1.0
    est = 2 * 3 * (_nbytes((tm, ATT_WIDTH), F32) + _nbytes((tm, LANES), F32)) + 6 * _nbytes((tm, ATT_WIDTH), F32)
    row = lambda i: (i, 0)
    fixed = lambda i: (0, 0)
    return pl.pallas_call(
        _attn_merge_kernel,
        grid=(n // tm,),
        in_specs=[pl.BlockSpec((tm, ATT_WIDTH), row)] * 3 + [pl.BlockSpec((tm, LANES), row)] * 3 + [
            pl.BlockSpec((LANES, ATT_WIDTH), fixed), pl.BlockSpec((1, ATT_WIDTH), fixed)],
        out_specs=pl.BlockSpec((tm, ATT_WIDTH), row),
        out_shape=jax.ShapeDtypeStruct((n, ATT_WIDTH), BF16),
        compiler_params=_params(("parallel",), est),
        name="attn_merge",
    )(*outs, *stats, jnp.asarray(expand), norm_w.reshape(1, ATT_WIDTH))


CONV_ROWS = 512
CONV_COLS = 256


def _conv_silu_kernel(x_ref, w_ref, b_ref, y_ref, *, seq):
    cb = pl.program_id(1)
    scale = jnp.where(cb >= MLSTM_WIDTH // CONV_COLS, MLSTM_HEAD_DIM ** -0.5, 1.0).astype(F32)
    w0, w1, w2 = w_ref[0:1, :], w_ref[1:2, :], w_ref[2:3, :]
    bias = b_ref[...]
    rows = lax.broadcasted_iota(jnp.int32, (CONV_ROWS, 1), 0)

    def body(c, carry):
        c0 = pl.multiple_of(c * CONV_ROWS, CONV_ROWS)
        xc = x_ref[0, pl.ds(c0, CONV_ROWS), :]
        before = x_ref[0, pl.ds(pl.multiple_of(jnp.maximum(c0 - 8, 0), 8), 8), :][7:8, :]
        after = x_ref[0, pl.ds(pl.multiple_of(jnp.minimum(c0 + CONV_ROWS, seq - 8), 8), 8), :][0:1, :]
        before = jnp.where(c0 > 0, before, 0.0)
        after = jnp.where(c0 + CONV_ROWS < seq, after, 0.0)
        up = jnp.where(rows == 0, before, pltpu.roll(xc, 1, axis=0))
        dn = jnp.where(rows == CONV_ROWS - 1, after, pltpu.roll(xc, CONV_ROWS - 1, axis=0))
        y = w0 * up + w1 * xc + w2 * dn + bias
        y_ref[0, pl.ds(c0, CONV_ROWS), :] = (y * jax.nn.sigmoid(y) * scale).astype(BF16)
        return carry

    lax.fori_loop(0, seq // CONV_ROWS, body, 0)


def _conv_silu(x, conv_w, conv_b):
    b, s, c = x.shape
    est = 2 * _nbytes((s, CONV_COLS), F32) + 2 * _nbytes((s, CONV_COLS), BF16) + 8 * _nbytes((CONV_ROWS, CONV_COLS), F32)
    return pl.pallas_call(
        functools.partial(_conv_silu_kernel, seq=s),
        grid=(b, c // CONV_COLS),
        in_specs=[
            pl.BlockSpec((1, s, CONV_COLS), lambda bi, cb: (bi, 0, cb)),
            pl.BlockSpec((3, CONV_COLS), lambda bi, cb: (0, cb)),
            pl.BlockSpec((1, CONV_COLS), lambda bi, cb: (0, cb)),
        ],
        out_specs=pl.BlockSpec((1, s, CONV_COLS), lambda bi, cb: (bi, 0, cb)),
        out_shape=jax.ShapeDtypeStruct((b, s, c), BF16),
        compiler_params=_params(("parallel", "parallel"), est),
        name="conv_silu",
    )(x, conv_w, conv_b.reshape(1, c))


EXT = MLSTM_HEAD_DIM + LANES


def _log_sigmoid(x):
    return jnp.minimum(x, 0.0) - jnp.log1p(jnp.exp(-jnp.abs(x)))


def _mlstm_kernel(*refs, reverse, tile):
    if reverse:
        q_ref, k_ref, v_ref, gt_ref, hf_ref, o_ref, nw_ref, out_ref, cext_ref, m_ref = refs
    else:
        q_ref, k_ref, v_ref, gt_ref, out_ref, cext_ref, m_ref = refs
    C = MLSTM_CHUNK
    nchunks = tile // C

    @pl.when(pl.program_id(1) == 0)
    def _():
        cext_ref[...] = jnp.zeros_like(cext_ref)
        m_ref[...] = jnp.full_like(m_ref, NEG)

    r_i = lax.broadcasted_iota(jnp.int32, (C, C), 0)
    c_i = lax.broadcasted_iota(jnp.int32, (C, C), 1)
    if reverse:
        causal, causal_t = c_i >= r_i, r_i >= c_i
    else:
        causal, causal_t = c_i <= r_i, r_i <= c_i
    tri_row = jnp.where(causal_t, 1.0, 0.0).astype(F32)
    tri_col = jnp.where(causal, 1.0, 0.0).astype(F32)
    eye = jnp.where(r_i == c_i, 1.0, 0.0).astype(F32)
    ones_lane = jnp.where(lax.broadcasted_iota(jnp.int32, (C, LANES), 1) == 0, 1.0, 0.0).astype(BF16)
    hi = lax.Precision.HIGHEST
    g_i = 2 * MLSTM_HEADS if reverse else 0
    g_f = g_i + MLSTM_HEADS

    def body(ci, carry):
        c = (nchunks - 1 - ci) if reverse else ci
        t0 = pl.multiple_of(c * C, C)
        gates = gt_ref[:, pl.ds(t0, C)]
        lsig = _log_sigmoid(gates)
        cum_rows = jnp.dot(lsig, tri_row, preferred_element_type=F32, precision=hi)
        cum_cols = lax.dot_general(tri_col, lsig, NT_DIMS, preferred_element_type=F32, precision=hi)
        gate_cols = lax.dot_general(eye, gates, NT_DIMS, preferred_element_type=F32, precision=hi)
        li, li_cols = gates[g_i:g_i + MLSTM_HEADS], gate_cols[:, g_i:g_i + MLSTM_HEADS]
        b_rows, b_cols = cum_rows[g_f:g_f + MLSTM_HEADS], cum_cols[:, g_f:g_f + MLSTM_HEADS]
        for hd in range(MLSTM_HEADS):
            hs = slice(hd * MLSTM_HEAD_DIM, (hd + 1) * MLSTM_HEAD_DIM)
            q = q_ref[0, pl.ds(t0, C), hs]
            k = k_ref[0, pl.ds(t0, C), hs]
            v = v_ref[0, pl.ds(t0, C), hs]
            vext = jnp.concatenate([v, ones_lane], axis=1)
            bc, br = b_cols[:, hd:hd + 1], b_rows[hd:hd + 1, :]
            lir, lic = li[hd:hd + 1, :], li_cols[:, hd:hd + 1]
            m_prev = m_ref[hd, 0:1, 0:1]
            dm = jnp.where(causal, bc - br + lir, -jnp.inf)
            inter = bc + m_prev
            m_t = jnp.maximum(inter, jnp.max(dm, axis=1, keepdims=True))
            sqk = lax.dot_general(q, k, NT_DIMS, preferred_element_type=F32)
            w_intra = (jnp.exp(dm - m_t) * sqk).astype(BF16)
            w_inter = jnp.exp(inter - m_t)
            cext = cext_ref[hd]
            num = (jnp.dot(w_intra, vext, preferred_element_type=F32)
                   + w_inter * jnp.dot(q, cext.astype(BF16), preferred_element_type=F32))
            den = num[:, MLSTM_HEAD_DIM:MLSTM_HEAD_DIM + 1]
            h = num[:, :MLSTM_HEAD_DIM] / jnp.maximum(jnp.abs(den), jnp.exp(-m_t))
            g = br[:, 0:1] if reverse else br[:, C - 1:C]
            m_new = jnp.maximum(g + m_prev, jnp.max(g - br + lir, axis=1, keepdims=True))
            w_a = jnp.exp(g - bc + lic - m_new)
            decay = jnp.exp(g + m_prev - m_new)
            vw = (vext.astype(F32) * w_a).astype(BF16)
            cext_ref[hd] = decay * cext + lax.dot_general(k, vw, TN_DIMS, preferred_element_type=F32)
            m_ref[hd] = jnp.broadcast_to(m_new, m_ref.shape[1:])
            if reverse:
                hsum = h + hf_ref[0, pl.ds(t0, C), hs]
                mu = jnp.mean(hsum, axis=1, keepdims=True)
                dev = hsum - mu
                var = jnp.mean(dev * dev, axis=1, keepdims=True)
                y = dev * lax.rsqrt(var + EPS) * nw_ref[:, hs] * jax.nn.sigmoid(o_ref[0, pl.ds(t0, C), hs])
                out_ref[0, pl.ds(t0, C), hs] = y.astype(BF16)
            else:
                out_ref[0, pl.ds(t0, C), hs] = h
        return carry

    lax.fori_loop(0, nchunks, body, 0)


def _mlstm(qk, v, gates_t, o_pre, norm_w, *, tile=512):
    b, s, _ = v.shape
    ns = s // tile
    w = MLSTM_WIDTH
    scratch = [pltpu.VMEM((MLSTM_HEADS, MLSTM_HEAD_DIM, EXT), F32), pltpu.VMEM((MLSTM_HEADS, 8, LANES), F32)]
    est = (2 * 3 * _nbytes((tile, w), BF16) + 2 * 3 * _nbytes((tile, w), F32)
           + _nbytes((MLSTM_HEADS, MLSTM_HEAD_DIM, EXT), F32) + 16 * _nbytes((MLSTM_CHUNK, EXT), F32))

    def specs(order):
        return [
            pl.BlockSpec((1, tile, w), lambda bi, si: (bi, order(si), 0)),
            pl.BlockSpec((1, tile, w), lambda bi, si: (bi, order(si), 1)),
            pl.BlockSpec((1, tile, w), lambda bi, si: (bi, order(si), 0)),
            pl.BlockSpec((N_GATES, tile), lambda bi, si: (0, bi * ns + order(si))),
        ]

    fwd = lambda si: si
    bwd = lambda si: ns - 1 - si
    h_fwd = pl.pallas_call(
        functools.partial(_mlstm_kernel, reverse=False, tile=tile),
        grid=(b, ns),
        in_specs=specs(fwd),
        out_specs=pl.BlockSpec((1, tile, w), lambda bi, si: (bi, si, 0)),
        out_shape=jax.ShapeDtypeStruct((b, s, w), F32),
        scratch_shapes=scratch,
        compiler_params=_params(("parallel", "arbitrary"), est),
        name="mlstm_fwd",
    )(qk, qk, v, gates_t)
    return pl.pallas_call(
        functools.partial(_mlstm_kernel, reverse=True, tile=tile),
        grid=(b, ns),
        in_specs=specs(bwd) + [
            pl.BlockSpec((1, tile, w), lambda bi, si: (bi, bwd(si), 0)),
            pl.BlockSpec((1, tile, w), lambda bi, si: (bi, bwd(si), 0)),
            pl.BlockSpec((1, w), lambda bi, si: (0, 0)),
        ],
        out_specs=pl.BlockSpec((1, tile, w), lambda bi, si: (bi, bwd(si), 0)),
        out_shape=jax.ShapeDtypeStruct((b, s, w), BF16),
        scratch_shapes=scratch,
        compiler_params=_params(("parallel", "arbitrary"), est),
        name="mlstm_bwd",
    )(qk, qk, v, gates_t, h_fwd, o_pre, norm_w.reshape(1, w))


def _out_proj_kernel(x_ref, ya_ref, ym_ref, wa_ref, wm_ref, o_ref):
    o_ref[...] = (x_ref[...]
                  + jnp.dot(ya_ref[...], wa_ref[...], preferred_element_type=F32)
                  + jnp.dot(ym_ref[...], wm_ref[...], preferred_element_type=F32))


def _out_proj(x, y_att, y_mem, w_out, *, tm=512):
    n, d = x.shape
    est = (2 * 2 * _nbytes((tm, d), F32) + 2 * 2 * _nbytes((tm, ATT_WIDTH), BF16)
           + 2 * _nbytes((d, d), BF16) + 2 * _nbytes((tm, d), F32))
    row = lambda i: (i, 0)
    return pl.pallas_call(
        _out_proj_kernel,
        grid=(n // tm,),
        in_specs=[
            pl.BlockSpec((tm, d), row),
            pl.BlockSpec((tm, ATT_WIDTH), row),
            pl.BlockSpec((tm, MLSTM_WIDTH), row),
            pl.BlockSpec((ATT_WIDTH, d), lambda i: (0, 0)),
            pl.BlockSpec((MLSTM_WIDTH, d), lambda i: (1, 0)),
        ],
        out_specs=pl.BlockSpec((tm, d), row),
        out_shape=jax.ShapeDtypeStruct((n, d), F32),
        compiler_params=_params(("parallel",), est),
        name="out_proj",
    )(x, y_att, y_mem, w_out, w_out)


def _ple_final_kernel(x_ref, p_ref, nw_ref, wg_ref, wp_ref, fw_ref, o_ref):
    x = x_ref[...]
    h = _rms(x, nw_ref[...]).astype(BF16)
    gate = jax.nn.sigmoid(jnp.dot(h, wg_ref[...], preferred_element_type=F32))
    proj = jnp.dot(p_ref[...].astype(BF16), wp_ref[...], preferred_element_type=F32)
    o_ref[...] = _rms(x + gate * proj, fw_ref[...])


def _ple_final(x, p, norm_w, w_gate, w_proj, final_w, *, tm=512):
    n, d = x.shape
    est = (2 * 2 * _nbytes((tm, d), F32) + 2 * _nbytes((tm, PLE_DIM), F32) + 2 * _nbytes((d, d), BF16)
           + 2 * _nbytes((PLE_DIM, d), BF16) + 4 * _nbytes((tm, d), F32))
    row = lambda i: (i, 0)
    fixed = lambda i: (0, 0)
    return pl.pallas_call(
        _ple_final_kernel,
        grid=(n // tm,),
        in_specs=[
            pl.BlockSpec((tm, d), row),
            pl.BlockSpec((tm, PLE_DIM), row),
            pl.BlockSpec((1, d), fixed),
            pl.BlockSpec((d, d), fixed),
            pl.BlockSpec((PLE_DIM, d), fixed),
            pl.BlockSpec((1, d), fixed),
        ],
        out_specs=pl.BlockSpec((tm, d), row),
        out_shape=jax.ShapeDtypeStruct((n, d), F32),
        compiler_params=_params(("parallel",), est),
        name="ple_final",
    )(x, p, norm_w.reshape(1, d), w_gate, w_proj, final_w.reshape(1, d))


def _trunk(x, p, wts):
    b, s, d = x.shape
    n = b * s
    h = _ffn(x.reshape(n, d), wts["ffn1_norm"], wts["ffn1_w_in"], wts["ffn1_w_out"])
    att, mqk, mv, mo, gates_t = _in_proj(h, wts["mix_norm"], wts["w_in_main"], wts["w_in_gates_t"], wts["b_gates"])
    att = att.reshape(b, s, 3 * ATT_WIDTH)
    outs, stats = zip(*[_attn_branch(att, wts["bias"][d_], d_) for d_ in DILATIONS])
    y_att = _attn_merge(outs, stats, wts["attn_out_norm"])
    qk = _conv_silu(mqk.reshape(b, s, 2 * MLSTM_WIDTH), wts["conv_w"], wts["conv_b"])
    y_mem = _mlstm(qk, mv.reshape(b, s, MLSTM_WIDTH), gates_t, mo.reshape(b, s, MLSTM_WIDTH),
                   wts["mlstm_out_norm"])
    h = _out_proj(h, y_att, y_mem.reshape(n, MLSTM_WIDTH), wts["w_out"])
    h = _ffn(h, wts["ffn2_norm"], wts["ffn2_w_in"], wts["ffn2_w_out"])
    y = _ple_final(h, p.reshape(n, PLE_DIM), wts["ple_norm"], wts["ple_w_gate"], wts["ple_w_proj"],
                   wts["final_norm"])
    return y.reshape(b, s, d)


def kernel(x_prompt, x_sample, p_prompt, p_sample, rel_table, ffn1_norm, ffn1_w_in, ffn1_w_out, mix_norm, w_in, b_gates, conv_w, conv_b, attn_out_norm, mlstm_out_norm, w_out, ffn2_norm, ffn2_w_in, ffn2_w_out, ple_norm, ple_w_gate, ple_w_proj, final_norm):
    depth = ffn1_norm.shape[0]
    assert depth == 1
    i = 0
    n_main = 3 * ATT_WIDTH + 4 * MLSTM_WIDTH
    col_scale = jnp.concatenate([jnp.full((ATT_WIDTH,), ATT_HEAD_DIM ** -0.5, F32),
                                 jnp.ones((n_main - ATT_WIDTH,), F32)])
    wts = dict(
        ffn1_norm=ffn1_norm[i], ffn1_w_in=ffn1_w_in[i].astype(BF16), ffn1_w_out=ffn1_w_out[i].astype(BF16),
        mix_norm=mix_norm[i],
        w_in_main=(w_in[i][:, :n_main] * col_scale).astype(BF16),
        w_in_gates_t=w_in[i][:, n_main:].T.astype(BF16),
        b_gates=b_gates[i], conv_w=conv_w[i], conv_b=conv_b[i],
        attn_out_norm=attn_out_norm[i], mlstm_out_norm=mlstm_out_norm[i],
        w_out=w_out[i].astype(BF16),
        ffn2_norm=ffn2_norm[i], ffn2_w_in=ffn2_w_in[i].astype(BF16), ffn2_w_out=ffn2_w_out[i].astype(BF16),
        ple_norm=ple_norm[i], ple_w_gate=ple_w_gate[i].astype(BF16), ple_w_proj=ple_w_proj[i].astype(BF16),
        final_norm=final_norm,
        bias={d_: _branch_bias(rel_table, d_) for d_ in DILATIONS},
    )
    return (_trunk(x_prompt, p_prompt[i], wts), _trunk(x_sample, p_sample[i], wts))
```

```python
import functools
import math

import jax
import jax.numpy as jnp
import numpy as np
from jax import lax
from jax.experimental import pallas as pl
from jax.experimental.pallas import tpu as pltpu

F32 = jnp.float32
BF16 = jnp.bfloat16

D_MODEL = 2048
ATT_HEADS = 16
ATT_HEAD_DIM = 64
ATT_WIDTH = ATT_HEADS * ATT_HEAD_DIM
DILATIONS = (1, 4, 16)
HALF_WINDOW = 64
N_BUCKETS = 32
MAX_DISTANCE = 1024
MLSTM_HEADS = 4
MLSTM_HEAD_DIM = 256
MLSTM_WIDTH = MLSTM_HEADS * MLSTM_HEAD_DIM
N_GATES = 4 * MLSTM_HEADS
D_FF = 5632
PLE_DIM = 256
EPS = 1e-6
NEG = -1e30
LOG2_E = math.log2(math.e)

LANES = 128
V7X_VMEM_BYTES = 64 * 1024 * 1024
VMEM_CAP = V7X_VMEM_BYTES - 6 * 1024 * 1024
VMEM_SLACK = 8 * 1024 * 1024

Q_BLOCK = 128
K_BLOCK = Q_BLOCK + 2 * HALF_WINDOW
ATT_UNROLL = 8
MLSTM_CHUNK = 128

NT_DIMS = (((1,), (1,)), ((), ()))
TN_DIMS = (((0,), (0,)), ((), ()))


def _params(semantics, est_bytes):
    limit = int(min(est_bytes + VMEM_SLACK, VMEM_CAP))
    return pltpu.CompilerParams(dimension_semantics=semantics, vmem_limit_bytes=limit)


def _nbytes(shape, dtype):
    return math.prod(shape) * jnp.dtype(dtype).itemsize


def _rms(x, w):
    ms = jnp.mean(x * x, axis=-1, keepdims=True)
    return x * lax.rsqrt(ms + EPS) * w


def _ffn_kernel(x_ref, nw_ref, wg_ref, wu_ref, wo_ref, o_ref, h_ref):
    f = pl.program_id(1)

    @pl.when(f == 0)
    def _():
        x = x_ref[...]
        h_ref[...] = _rms(x, nw_ref[...]).astype(BF16)
        o_ref[...] = x

    h = h_ref[...]
    g = jnp.dot(h, wg_ref[...], preferred_element_type=F32)
    u = jnp.dot(h, wu_ref[...], preferred_element_type=F32)
    a = (g * jax.nn.sigmoid(g) * u * 0.5).astype(BF16)
    o_ref[...] += jnp.dot(a, wo_ref[...], preferred_element_type=F32)


def _ffn(x, norm_w, w_in, w_out, *, tm=512, tf=512):
    n, d = x.shape
    nf = D_FF // tf
    est = (2 * 2 * _nbytes((tm, d), F32) + _nbytes((tm, d), BF16)
           + 2 * 3 * _nbytes((d, tf), BF16) + 3 * _nbytes((tm, tf), F32))
    return pl.pallas_call(
        _ffn_kernel,
        grid=(n // tm, nf),
        in_specs=[
            pl.BlockSpec((tm, d), lambda i, f: (i, 0)),
            pl.BlockSpec((1, d), lambda i, f: (0, 0)),
            pl.BlockSpec((d, tf), lambda i, f: (0, f)),
            pl.BlockSpec((d, tf), lambda i, f: (0, f + nf)),
            pl.BlockSpec((tf, d), lambda i, f: (f, 0)),
        ],
        out_specs=pl.BlockSpec((tm, d), lambda i, f: (i, 0)),
        out_shape=jax.ShapeDtypeStruct((n, d), F32),
        scratch_shapes=[pltpu.VMEM((tm, d), BF16)],
        compiler_params=_params(("parallel", "arbitrary"), est),
        name="ffn",
    )(x, norm_w.reshape(1, d), w_in, w_in, w_out)


IN_TN = 512
NB_ATT = 3 * ATT_WIDTH // IN_TN
NB_MQK = 2 * MLSTM_WIDTH // IN_TN
NB_MV = MLSTM_WIDTH // IN_TN
NB_MO = MLSTM_WIDTH // IN_TN


def _in_proj_kernel(x_ref, nw_ref, w_ref, wgt_ref, bg_ref,
                    att_ref, mqk_ref, mv_ref, mo_ref, gt_ref, h_ref):
    j = pl.program_id(1)

    @pl.when(j == 0)
    def _():
        h = _rms(x_ref[...], nw_ref[...]).astype(BF16)
        h_ref[...] = h
        gt_ref[...] = lax.dot_general(wgt_ref[...], h, NT_DIMS, preferred_element_type=F32) + bg_ref[...]

    z = jnp.dot(h_ref[...], w_ref[...], preferred_element_type=F32)

    @pl.when(j < NB_ATT)
    def _():
        att_ref[...] = z.astype(BF16)

    @pl.when((j >= NB_ATT) & (j < NB_ATT + NB_MQK))
    def _():
        mqk_ref[...] = z

    @pl.when((j >= NB_ATT + NB_MQK) & (j < NB_ATT + NB_MQK + NB_MV))
    def _():
        mv_ref[...] = z.astype(BF16)

    @pl.when(j >= NB_ATT + NB_MQK + NB_MV)
    def _():
        mo_ref[...] = z


def _in_proj(x, norm_w, w_main, w_gates_t, b_gates, *, tm=1024):
    n, d = x.shape
    tn = IN_TN
    nb = NB_ATT + NB_MQK + NB_MV + NB_MO
    o1, o2, o3 = NB_ATT, NB_ATT + NB_MQK, NB_ATT + NB_MQK + NB_MV

    def col(off, cnt):
        return lambda i, j: (i, jnp.clip(j - off, 0, cnt - 1))

    est = (2 * _nbytes((tm, d), F32) + _nbytes((tm, d), BF16) + 2 * _nbytes((d, tn), BF16)
           + 2 * 2 * (_nbytes((tm, tn), F32) + _nbytes((tm, tn), BF16)) + 2 * _nbytes((tm, tn), F32))
    return pl.pallas_call(
        _in_proj_kernel,
        grid=(n // tm, nb),
        in_specs=[
            pl.BlockSpec((tm, d), lambda i, j: (i, 0)),
            pl.BlockSpec((1, d), lambda i, j: (0, 0)),
            pl.BlockSpec((d, tn), lambda i, j: (0, j)),
            pl.BlockSpec((N_GATES, d), lambda i, j: (0, 0)),
            pl.BlockSpec((N_GATES, 1), lambda i, j: (0, 0)),
        ],
        out_specs=[
            pl.BlockSpec((tm, tn), col(0, NB_ATT)),
            pl.BlockSpec((tm, tn), col(o1, NB_MQK)),
            pl.BlockSpec((tm, tn), col(o2, NB_MV)),
            pl.BlockSpec((tm, tn), col(o3, NB_MO)),
            pl.BlockSpec((N_GATES, tm), lambda i, j: (0, i)),
        ],
        out_shape=[
            jax.ShapeDtypeStruct((n, 3 * ATT_WIDTH), BF16),
            jax.ShapeDtypeStruct((n, 2 * MLSTM_WIDTH), F32),
            jax.ShapeDtypeStruct((n, MLSTM_WIDTH), BF16),
            jax.ShapeDtypeStruct((n, MLSTM_WIDTH), F32),
            jax.ShapeDtypeStruct((N_GATES, n), F32),
        ],
        scratch_shapes=[pltpu.VMEM((tm, d), BF16)],
        compiler_params=_params(("parallel", "arbitrary"), est),
        name="in_proj",
    )(x, norm_w.reshape(1, d), w_main, w_gates_t, b_gates.reshape(N_GATES, 1))


def _rel_bucket(rel):
    nb = N_BUCKETS // 2
    max_exact = nb // 2
    n = np.abs(rel)
    large = max_exact + (np.log(np.maximum(n, 1) / max_exact) / math.log(MAX_DISTANCE / max_exact)
                         * (nb - max_exact)).astype(np.int32)
    large = np.minimum(large, nb - 1)
    return np.where(rel > 0, nb, 0) + np.where(n < max_exact, n, large)


def _branch_bias(rel_table, dilation):
    key = np.arange(K_BLOCK)[None, :]
    off = key - HALF_WINDOW - np.arange(Q_BLOCK)[:, None]
    band = np.abs(off) <= HALF_WINDOW
    keep = np.stack([band & ((v & 1 == 0) | (key >= HALF_WINDOW)) & ((v & 2 == 0) | (key < K_BLOCK - HALF_WINDOW))
                     for v in range(4)])
    onehot = (_rel_bucket(off * dilation).reshape(1, -1) == np.arange(N_BUCKETS)[:, None]).astype(np.float32)
    bias = jnp.einsum("bh,bn->hn", rel_table.astype(F32) * LOG2_E, jnp.asarray(onehot),
                      precision=lax.Precision.HIGHEST)
    return jnp.where(keep[:, None], bias.reshape(1, ATT_HEADS, Q_BLOCK, K_BLOCK), NEG)


def _attn_branch_kernel(q_ref, k_ref, v_ref, bias_ref, o_ref, st_ref, kpad, vpad, *, seq, width):
    pw = pl.program_id(2)
    pairs = width // LANES

    zeros = jnp.zeros((HALF_WINDOW, width), BF16)
    kpad[0:HALF_WINDOW, :] = zeros
    vpad[0:HALF_WINDOW, :] = zeros
    kpad[HALF_WINDOW + seq:2 * HALF_WINDOW + seq, :] = zeros
    vpad[HALF_WINDOW + seq:2 * HALF_WINDOW + seq, :] = zeros
    kpad[HALF_WINDOW:HALF_WINDOW + seq, :] = k_ref[0]
    vpad[HALF_WINDOW:HALF_WINDOW + seq, :] = v_ref[0]

    @pl.when(pw == 0)
    def _():
        st_ref[...] = jnp.zeros_like(st_ref)

    lane = lax.broadcasted_iota(jnp.int32, (1, LANES), 1)
    low = lane < ATT_HEAD_DIM
    n_blocks = seq // Q_BLOCK
    unroll = math.gcd(n_blocks, ATT_UNROLL)

    for p in range(pairs):
        sl = slice(p * LANES, (p + 1) * LANES)
        head_pair = pw * pairs + p

        def block(q0, sl=sl, p=p, head_pair=head_pair):
            qb = q_ref[0, pl.ds(q0, Q_BLOCK), sl]
            kw = kpad[pl.ds(q0, K_BLOCK), sl]
            vw = vpad[pl.ds(q0, K_BLOCK), sl]
            var = jnp.where(q0 == 0, 1, 0) + jnp.where(q0 == seq - Q_BLOCK, 2, 0)
            zero = jnp.zeros_like(qb)
            qs = jnp.concatenate([jnp.where(low, qb, zero), jnp.where(low, zero, qb)], axis=0)
            s = lax.dot_general(qs, kw, NT_DIMS, preferred_element_type=F32)
            s0 = s[:Q_BLOCK] + bias_ref[var, 2 * p]
            s1 = s[Q_BLOCK:] + bias_ref[var, 2 * p + 1]
            m0 = jnp.max(s0, axis=1, keepdims=True)
            m1 = jnp.max(s1, axis=1, keepdims=True)
            e0 = jnp.exp2(s0 - m0)
            e1 = jnp.exp2(s1 - m1)
            l0 = jnp.sum(e0, axis=1, keepdims=True)
            l1 = jnp.sum(e1, axis=1, keepdims=True)
            e = jnp.concatenate([e0, e1], axis=0).astype(BF16)
            pv = jnp.dot(e, vw, preferred_element_type=F32)
            o_ref[0, pl.ds(q0, Q_BLOCK), sl] = jnp.where(low, pv[:Q_BLOCK] / l0, pv[Q_BLOCK:] / l1)
            c0 = m0 + jnp.log2(l0)
            c1 = m1 + jnp.log2(l1)
            prev = st_ref[0, pl.ds(q0, Q_BLOCK), :]
            st_ref[0, pl.ds(q0, Q_BLOCK), :] = jnp.where(
                lane == 2 * head_pair, c0, jnp.where(lane == 2 * head_pair + 1, c1, prev))

        def body(i, carry, block=block):
            for u in range(unroll):
                block(pl.multiple_of((i * unroll + u) * Q_BLOCK, Q_BLOCK))
            return carry

        lax.fori_loop(0, n_blocks // unroll, body, 0)


def _attn_width(seq):
    return int(min(ATT_WIDTH, max(LANES, (1 << 20) // seq // LANES * LANES)))


def _attn_branch(att, bias, dilation):
    b, s, _ = att.shape
    d = dilation
    seq = s // d
    width = _attn_width(seq)
    npw = ATT_WIDTH // width
    cols = 3 * ATT_WIDTH // width
    att_v = att.reshape(b, seq, d * 3 * ATT_WIDTH)
    est = (2 * 3 * _nbytes((seq, width), BF16) + 2 * _nbytes((4, width // ATT_HEAD_DIM, Q_BLOCK, K_BLOCK), F32)
           + 2 * _nbytes((seq, width), F32) + 2 * _nbytes((seq, LANES), F32)
           + 2 * _nbytes((seq + 2 * HALF_WINDOW, width), BF16))
    kern = functools.partial(_attn_branch_kernel, seq=seq, width=width)
    o, st = pl.pallas_call(
        kern,
        grid=(b, d, npw),
        in_specs=[
            pl.BlockSpec((1, seq, width), lambda bi, r, pw: (bi, 0, r * cols + pw)),
            pl.BlockSpec((1, seq, width), lambda bi, r, pw: (bi, 0, r * cols + npw + pw)),
            pl.BlockSpec((1, seq, width), lambda bi, r, pw: (bi, 0, r * cols + 2 * npw + pw)),
            pl.BlockSpec((4, width // ATT_HEAD_DIM, Q_BLOCK, K_BLOCK), lambda bi, r, pw: (0, pw, 0, 0)),
        ],
        out_specs=[
            pl.BlockSpec((1, seq, width), lambda bi, r, pw: (bi, 0, r * npw + pw)),
            pl.BlockSpec((1, seq, LANES), lambda bi, r, pw: (bi, 0, r)),
        ],
        out_shape=[
            jax.ShapeDtypeStruct((b, seq, d * ATT_WIDTH), F32),
            jax.ShapeDtypeStruct((b, seq, d * LANES), F32),
        ],
        scratch_shapes=[pltpu.VMEM((seq + 2 * HALF_WINDOW, width), BF16),
                        pltpu.VMEM((seq + 2 * HALF_WINDOW, width), BF16)],
        compiler_params=_params(("parallel", "parallel", "arbitrary"), est),
        name=f"attn_d{d}",
    )(att_v, att_v, att_v, bias)
    return o.reshape(b * s, ATT_WIDTH), st.reshape(b * s, LANES)


def _attn_merge_kernel(o1_ref, o2_ref, o3_ref, s1_ref, s2_ref, s3_ref, ex_ref, nw_ref, y_ref):
    lane = lax.broadcasted_iota(jnp.int32, (1, LANES), 1)
    c1, c2, c3 = s1_ref[...], s2_ref[...], s3_ref[...]
    cm = jnp.maximum(jnp.maximum(c1, c2), c3)
    w1, w2, w3 = jnp.exp2(c1 - cm), jnp.exp2(c2 - cm), jnp.exp2(c3 - cm)
    inv = 1.0 / (w1 + w2 + w3)
    head = lane < ATT_HEADS
    ex = ex_ref[...]

    def spread(w):
        return jnp.dot(jnp.where(head, w * inv, 0.0), ex, preferred_element_type=F32,
                       precision=lax.Precision.HIGHEST)

    out = spread(w1) * o1_ref[...] + spread(w2) * o2_ref[...] + spread(w3) * o3_ref[...]
    y_ref[...] = _rms(out, nw_ref[...]).astype(BF16)


def _attn_merge(outs, stats, norm_w, *, tm=512):
    n = outs[0].shape[0]
    expand = np.zeros((LANES, ATT_WIDTH), np.float32)
    for h in range(ATT_HEADS):
        expand[h, h * ATT_HEAD_DIM:(h + 1) * ATT_HEAD_DIM] = 1.0
    est = 2 * 3 * (_nbytes((tm, ATT_WIDTH), F32) + _nbytes((tm, LANES), F32)) + 6 * _nbytes((tm, ATT_WIDTH), F32)
    row = lambda i: (i, 0)
    fixed = lambda i: (0, 0)
    return pl.pallas_call(
        _attn_merge_kernel,
        grid=(n // tm,),
        in_specs=[pl.BlockSpec((tm, ATT_WIDTH), row)] * 3 + [pl.BlockSpec((tm, LANES), row)] * 3 + [
            pl.BlockSpec((LANES, ATT_WIDTH), fixed), pl.BlockSpec((1, ATT_WIDTH), fixed)],
        out_specs=pl.BlockSpec((tm, ATT_WIDTH), row),
        out_shape=jax.ShapeDtypeStruct((n, ATT_WIDTH), BF16),
        compiler_params=_params(("parallel",), est),
        name="attn_merge",
    )(*outs, *stats, jnp.asarray(expand), norm_w.reshape(1, ATT_WIDTH))


CONV_ROWS = 512
CONV_COLS = 256


def _conv_silu_kernel(x_ref, w_ref, b_ref, y_ref, *, seq):
    cb = pl.program_id(1)
    scale = jnp.where(cb >= MLSTM_WIDTH // CONV_COLS, MLSTM_HEAD_DIM ** -0.5, 1.0).astype(F32)
    w0, w1, w2 = w_ref[0:1, :], w_ref[1:2, :], w_ref[2:3, :]
    bias = b_ref[...]
    rows = lax.broadcasted_iota(jnp.int32, (CONV_ROWS, 1), 0)

    def body(c, carry):
        c0 = pl.multiple_of(c * CONV_ROWS, CONV_ROWS)
        xc = x_ref[0, pl.ds(c0, CONV_ROWS), :]
        before = x_ref[0, pl.ds(pl.multiple_of(jnp.maximum(c0 - 8, 0), 8), 8), :][7:8, :]
        after = x_ref[0, pl.ds(pl.multiple_of(jnp.minimum(c0 + CONV_ROWS, seq - 8), 8), 8), :][0:1, :]
        before = jnp.where(c0 > 0, before, 0.0)
        after = jnp.where(c0 + CONV_ROWS < seq, after, 0.0)
        up = jnp.where(rows == 0, before, pltpu.roll(xc, 1, axis=0))
        dn = jnp.where(rows == CONV_ROWS - 1, after, pltpu.roll(xc, CONV_ROWS - 1, axis=0))
        y = w0 * up + w1 * xc + w2 * dn + bias
        y_ref[0, pl.ds(c0, CONV_ROWS), :] = (y * jax.nn.sigmoid(y) * scale).astype(BF16)
        return carry

    lax.fori_loop(0, seq // CONV_ROWS, body, 0)


def _conv_silu(x, conv_w, conv_b):
    b, s, c = x.shape
    est = 2 * _nbytes((s, CONV_COLS), F32) + 2 * _nbytes((s, CONV_COLS), BF16) + 8 * _nbytes((CONV_ROWS, CONV_COLS), F32)
    return pl.pallas_call(
        functools.partial(_conv_silu_kernel, seq=s),
        grid=(b, c // CONV_COLS),
        in_specs=[
            pl.BlockSpec((1, s, CONV_COLS), lambda bi, cb: (bi, 0, cb)),
            pl.BlockSpec((3, CONV_COLS), lambda bi, cb: (0, cb)),
            pl.BlockSpec((1, CONV_COLS), lambda bi, cb: (0, cb)),
        ],
        out_specs=pl.BlockSpec((1, s, CONV_COLS), lambda bi, cb: (bi, 0, cb)),
        out_shape=jax.ShapeDtypeStruct((b, s, c), BF16),
        compiler_params=_params(("parallel", "parallel"), est),
        name="conv_silu",
    )(x, conv_w, conv_b.reshape(1, c))


EXT = MLSTM_HEAD_DIM + LANES


def _log_sigmoid(x):
    return jnp.minimum(x, 0.0) - jnp.log1p(jnp.exp(-jnp.abs(x)))


def _mlstm_kernel(*refs, reverse, tile):
    if reverse:
        q_ref, k_ref, v_ref, gt_ref, hf_ref, o_ref, nw_ref, out_ref, cext_ref, m_ref = refs
    else:
        q_ref, k_ref, v_ref, gt_ref, out_ref, cext_ref, m_ref = refs
    C = MLSTM_CHUNK
    nchunks = tile // C

    @pl.when(pl.program_id(1) == 0)
    def _():
        cext_ref[...] = jnp.zeros_like(cext_ref)
        m_ref[...] = jnp.full_like(m_ref, NEG)

    r_i = lax.broadcasted_iota(jnp.int32, (C, C), 0)
    c_i = lax.broadcasted_iota(jnp.int32, (C, C), 1)
    if reverse:
        causal, causal_t = c_i >= r_i, r_i >= c_i
    else:
        causal, causal_t = c_i <= r_i, r_i <= c_i
    tri_row = jnp.where(causal_t, 1.0, 0.0).astype(F32)
    tri_col = jnp.where(causal, 1.0, 0.0).astype(F32)
    eye = jnp.where(r_i == c_i, 1.0, 0.0).astype(F32)
    ones_lane = jnp.where(lax.broadcasted_iota(jnp.int32, (C, LANES), 1) == 0, 1.0, 0.0).astype(BF16)
    hi = lax.Precision.HIGHEST
    g_i = 2 * MLSTM_HEADS if reverse else 0
    g_f = g_i + MLSTM_HEADS

    def body(ci, carry):
        c = (nchunks - 1 - ci) if reverse else ci
        t0 = pl.multiple_of(c * C, C)
        gates = gt_ref[:, pl.ds(t0, C)]
        lsig = _log_sigmoid(gates)
        cum_rows = jnp.dot(lsig, tri_row, preferred_element_type=F32, precision=hi)
        cum_cols = lax.dot_general(tri_col, lsig, NT_DIMS, preferred_element_type=F32, precision=hi)
        gate_cols = lax.dot_general(eye, gates, NT_DIMS, preferred_element_type=F32, precision=hi)
        li, li_cols = gates[g_i:g_i + MLSTM_HEADS], gate_cols[:, g_i:g_i + MLSTM_HEADS]
        b_rows, b_cols = cum_rows[g_f:g_f + MLSTM_HEADS], cum_cols[:, g_f:g_f + MLSTM_HEADS]
        for hd in range(MLSTM_HEADS):
            hs = slice(hd * MLSTM_HEAD_DIM, (hd + 1) * MLSTM_HEAD_DIM)
            q = q_ref[0, pl.ds(t0, C), hs]
            k = k_ref[0, pl.ds(t0, C), hs]
            v = v_ref[0, pl.ds(t0, C), hs]
            vext = jnp.concatenate([v, ones_lane], axis=1)
            bc, br = b_cols[:, hd:hd + 1], b_rows[hd:hd + 1, :]
            lir, lic = li[hd:hd + 1, :], li_cols[:, hd:hd + 1]
            m_prev = m_ref[hd, 0:1, 0:1]
            dm = jnp.where(causal, bc - br + lir, -jnp.inf)
            inter = bc + m_prev
            m_t = jnp.maximum(inter, jnp.max(dm, axis=1, keepdims=True))
            sqk = lax.dot_general(q, k, NT_DIMS, preferred_element_type=F32)
            w_intra = (jnp.exp(dm - m_t) * sqk).astype(BF16)
            w_inter = jnp.exp(inter - m_t)
            cext = cext_ref[hd]
            num = (jnp.dot(w_intra, vext, preferred_element_type=F32)
                   + w_inter * jnp.dot(q, cext.astype(BF16), preferred_element_type=F32))
            den = num[:, MLSTM_HEAD_DIM:MLSTM_HEAD_DIM + 1]
            h = num[:, :MLSTM_HEAD_DIM] / jnp.maximum(jnp.abs(den), jnp.exp(-m_t))
            g = br[:, 0:1] if reverse else br[:, C - 1:C]
            m_new = jnp.maximum(g + m_prev, jnp.max(g - br + lir, axis=1, keepdims=True))
            w_a = jnp.exp(g - bc + lic - m_new)
            decay = jnp.exp(g + m_prev - m_new)
            vw = (vext.astype(F32) * w_a).astype(BF16)
            cext_ref[hd] = decay * cext + lax.dot_general(k, vw, TN_DIMS, preferred_element_type=F32)
            m_ref[hd] = jnp.broadcast_to(m_new, m_ref.shape[1:])
            if reverse:
                hsum = h + hf_ref[0, pl.ds(t0, C), hs]
                mu = jnp.mean(hsum, axis=1, keepdims=True)
                dev = hsum - mu
                var = jnp.mean(dev * dev, axis=1, keepdims=True)
                y = dev * lax.rsqrt(var + EPS) * nw_ref[:, hs] * jax.nn.sigmoid(o_ref[0, pl.ds(t0, C), hs])
                out_ref[0, pl.ds(t0, C), hs] = y.astype(BF16)
            else:
                out_ref[0, pl.ds(t0, C), hs] = h
        return carry

    lax.fori_loop(0, nchunks, body, 0)


def _mlstm(qk, v, gates_t, o_pre, norm_w, *, tile=512):
    b, s, _ = v.shape
    ns = s // tile
    w = MLSTM_WIDTH
    scratch = [pltpu.VMEM((MLSTM_HEADS, MLSTM_HEAD_DIM, EXT), F32), pltpu.VMEM((MLSTM_HEADS, 8, LANES), F32)]
    est = (2 * 3 * _nbytes((tile, w), BF16) + 2 * 3 * _nbytes((tile, w), F32)
           + _nbytes((MLSTM_HEADS, MLSTM_HEAD_DIM, EXT), F32) + 16 * _nbytes((MLSTM_CHUNK, EXT), F32))

    def specs(order):
        return [
            pl.BlockSpec((1, tile, w), lambda bi, si: (bi, order(si), 0)),
            pl.BlockSpec((1, tile, w), lambda bi, si: (bi, order(si), 1)),
            pl.BlockSpec((1, tile, w), lambda bi, si: (bi, order(si), 0)),
            pl.BlockSpec((N_GATES, tile), lambda bi, si: (0, bi * ns + order(si))),
        ]

    fwd = lambda si: si
    bwd = lambda si: ns - 1 - si
    h_fwd = pl.pallas_call(
        functools.partial(_mlstm_kernel, reverse=False, tile=tile),
        grid=(b, ns),
        in_specs=specs(fwd),
        out_specs=pl.BlockSpec((1, tile, w), lambda bi, si: (bi, si, 0)),
        out_shape=jax.ShapeDtypeStruct((b, s, w), F32),
        scratch_shapes=scratch,
        compiler_params=_params(("parallel", "arbitrary"), est),
        name="mlstm_fwd",
    )(qk, qk, v, gates_t)
    return pl.pallas_call(
        functools.partial(_mlstm_kernel, reverse=True, tile=tile),
        grid=(b, ns),
        in_specs=specs(bwd) + [
            pl.BlockSpec((1, tile, w), lambda bi, si: (bi, bwd(si), 0)),
            pl.BlockSpec((1, tile, w), lambda bi, si: (bi, bwd(si), 0)),
            pl.BlockSpec((1, w), lambda bi, si: (0, 0)),
        ],
        out_specs=pl.BlockSpec((1, tile, w), lambda bi, si: (bi, bwd(si), 0)),
        out_shape=jax.ShapeDtypeStruct((b, s, w), BF16),
        scratch_shapes=scratch,
        compiler_params=_params(("parallel", "arbitrary"), est),
        name="mlstm_bwd",
    )(qk, qk, v, gates_t, h_fwd, o_pre, norm_w.reshape(1, w))


def _out_proj_kernel(x_ref, ya_ref, ym_ref, wa_ref, wm_ref, o_ref):
    o_ref[...] = (x_ref[...]
                  + jnp.dot(ya_ref[...], wa_ref[...], preferred_element_type=F32)
                  + jnp.dot(ym_ref[...], wm_ref[...], preferred_element_type=F32))


def _out_proj(x, y_att, y_mem, w_out, *, tm=512):
    n, d = x.shape
    est = (2 * 2 * _nbytes((tm, d), F32) + 2 * 2 * _nbytes((tm, ATT_WIDTH), BF16)
           + 2 * _nbytes((d, d), BF16) + 2 * _nbytes((tm, d), F32))
    row = lambda i: (i, 0)
    return pl.pallas_call(
        _out_proj_kernel,
        grid=(n // tm,),
        in_specs=[
            pl.BlockSpec((tm, d), row),
            pl.BlockSpec((tm, ATT_WIDTH), row),
            pl.BlockSpec((tm, MLSTM_WIDTH), row),
            pl.BlockSpec((ATT_WIDTH, d), lambda i: (0, 0)),
            pl.BlockSpec((MLSTM_WIDTH, d), lambda i: (1, 0)),
        ],
        out_specs=pl.BlockSpec((tm, d), row),
        out_shape=jax.ShapeDtypeStruct((n, d), F32),
        compiler_params=_params(("parallel",), est),
        name="out_proj",
    )(x, y_att, y_mem, w_out, w_out)


def _ple_final_kernel(x_ref, p_ref, nw_ref, wg_ref, wp_ref, fw_ref, o_ref):
    x = x_ref[...]
    h = _rms(x, nw_ref[...]).astype(BF16)
    gate = jax.nn.sigmoid(jnp.dot(h, wg_ref[...], preferred_element_type=F32))
    proj = jnp.dot(p_ref[...].astype(BF16), wp_ref[...], preferred_element_type=F32)
    o_ref[...] = _rms(x + gate * proj, fw_ref[...])


def _ple_final(x, p, norm_w, w_gate, w_proj, final_w, *, tm=512):
    n, d = x.shape
    est = (2 * 2 * _nbytes((tm, d), F32) + 2 * _nbytes((tm, PLE_DIM), F32) + 2 * _nbytes((d, d), BF16)
           + 2 * _nbytes((PLE_DIM, d), BF16) + 4 * _nbytes((tm, d), F32))
    row = lambda i: (i, 0)
    fixed = lambda i: (0, 0)
    return pl.pallas_call(
        _ple_final_kernel,
        grid=(n // tm,),
        in_specs=[
            pl.BlockSpec((tm, d), row),
            pl.BlockSpec((tm, PLE_DIM), row),
            pl.BlockSpec((1, d), fixed),
            pl.BlockSpec((d, d), fixed),
            pl.BlockSpec((PLE_DIM, d), fixed),
            pl.BlockSpec((1, d), fixed),
        ],
        out_specs=pl.BlockSpec((tm, d), row),
        out_shape=jax.ShapeDtypeStruct((n, d), F32),
        compiler_params=_params(("parallel",), est),
        name="ple_final",
    )(x, p, norm_w.reshape(1, d), w_gate, w_proj, final_w.reshape(1, d))


def _trunk(x, p, wts):
    b, s, d = x.shape
    n = b * s
    h = _ffn(x.reshape(n, d), wts["ffn1_norm"], wts["ffn1_w_in"], wts["ffn1_w_out"])
    att, mqk, mv, mo, gates_t = _in_proj(h, wts["mix_norm"], wts["w_in_main"], wts["w_in_gates_t"], wts["b_gates"])
    att = att.reshape(b, s, 3 * ATT_WIDTH)
    outs, stats = zip(*[_attn_branch(att, wts["bias"][d_], d_) for d_ in DILATIONS])
    y_att = _attn_merge(outs, stats, wts["attn_out_norm"])
    qk = _conv_silu(mqk.reshape(b, s, 2 * MLSTM_WIDTH), wts["conv_w"], wts["conv_b"])
    y_mem = _mlstm(qk, mv.reshape(b, s, MLSTM_WIDTH), gates_t, mo.reshape(b, s, MLSTM_WIDTH),
                   wts["mlstm_out_norm"])
    h = _out_proj(h, y_att, y_mem.reshape(n, MLSTM_WIDTH), wts["w_out"])
    h = _ffn(h, wts["ffn2_norm"], wts["ffn2_w_in"], wts["ffn2_w_out"])
    y = _ple_final(h, p.reshape(n, PLE_DIM), wts["ple_norm"], wts["ple_w_gate"], wts["ple_w_proj"],
                   wts["final_norm"])
    return y.reshape(b, s, d)


def kernel(x_prompt, x_sample, p_prompt, p_sample, rel_table, ffn1_norm, ffn1_w_in, ffn1_w_out, mix_norm, w_in, b_gates, conv_w, conv_b, attn_out_norm, mlstm_out_norm, w_out, ffn2_norm, ffn2_w_in, ffn2_w_out, ple_norm, ple_w_gate, ple_w_proj, final_norm):
    depth = ffn1_norm.shape[0]
    assert depth == 1
    i = 0
    n_main = 3 * ATT_WIDTH + 4 * MLSTM_WIDTH
    col_scale = jnp.concatenate([jnp.full((ATT_WIDTH,), ATT_HEAD_DIM ** -0.5 * LOG2_E, F32),
                                 jnp.ones((n_main - ATT_WIDTH,), F32)])
    wts = dict(
        ffn1_norm=ffn1_norm[i], ffn1_w_in=ffn1_w_in[i].astype(BF16), ffn1_w_out=ffn1_w_out[i].astype(BF16),
        mix_norm=mix_norm[i],
        w_in_main=(w_in[i][:, :n_main] * col_scale).astype(BF16),
        w_in_gates_t=w_in[i][:, n_main:].T.astype(BF16),
        b_gates=b_gates[i], conv_w=conv_w[i], conv_b=conv_b[i],
        attn_out_norm=attn_out_norm[i], mlstm_out_norm=mlstm_out_norm[i],
        w_out=w_out[i].astype(BF16),
        ffn2_norm=ffn2_norm[i], ffn2_w_in=ffn2_w_in[i].astype(BF16), ffn2_w_out=ffn2_w_out[i].astype(BF16),
        ple_norm=ple_norm[i], ple_w_gate=ple_w_gate[i].astype(BF16), ple_w_proj=ple_w_proj[i].astype(BF16),
        final_norm=final_norm,
        bias={d_: _branch_bias(rel_table, d_) for d_ in DILATIONS},
    )
    return (_trunk(x_prompt, p_prompt[i], wts), _trunk(x_sample, p_sample[i], wts))
```

```python
import functools
import math

import jax
import jax.numpy as jnp
import numpy as np
from jax import lax
from jax.experimental import pallas as pl
from jax.experimental.pallas import tpu as pltpu

F32 = jnp.float32
BF16 = jnp.bfloat16

D_MODEL = 2048
ATT_HEADS = 16
ATT_HEAD_DIM = 64
ATT_WIDTH = ATT_HEADS * ATT_HEAD_DIM
DILATIONS = (1, 4, 16)
HALF_WINDOW = 64
N_BUCKETS = 32
MAX_DISTANCE = 1024
MLSTM_HEADS = 4
MLSTM_HEAD_DIM = 256
MLSTM_WIDTH = MLSTM_HEADS * MLSTM_HEAD_DIM
N_GATES = 4 * MLSTM_HEADS
D_FF = 5632
PLE_DIM = 256
EPS = 1e-6
NEG = -1e30
LOG2_E = math.log2(math.e)

LANES = 128
V7X_VMEM_BYTES = 64 * 1024 * 1024
VMEM_CAP = V7X_VMEM_BYTES - 6 * 1024 * 1024
VMEM_SLACK = 8 * 1024 * 1024

Q_BLOCK = 128
K_BLOCK = Q_BLOCK + 2 * HALF_WINDOW
ATT_UNROLL = 8
MLSTM_CHUNK = 128

NT_DIMS = (((1,), (1,)), ((), ()))
TN_DIMS = (((0,), (0,)), ((), ()))


def _params(semantics, est_bytes):
    limit = int(min(est_bytes + VMEM_SLACK, VMEM_CAP))
    return pltpu.CompilerParams(dimension_semantics=semantics, vmem_limit_bytes=limit)


def _nbytes(shape, dtype):
    return math.prod(shape) * jnp.dtype(dtype).itemsize


def _rms(x, w):
    ms = jnp.mean(x * x, axis=-1, keepdims=True)
    return x * lax.rsqrt(ms + EPS) * w


def _ffn_kernel(x_ref, nw_ref, wg_ref, wu_ref, wo_ref, o_ref, h_ref):
    f = pl.program_id(1)

    @pl.when(f == 0)
    def _():
        x = x_ref[...]
        h_ref[...] = _rms(x, nw_ref[...]).astype(BF16)
        o_ref[...] = x

    h = h_ref[...]
    g = jnp.dot(h, wg_ref[...], preferred_element_type=F32)
    u = jnp.dot(h, wu_ref[...], preferred_element_type=F32)
    a = (g * jax.nn.sigmoid(g) * u * 0.5).astype(BF16)
    o_ref[...] += jnp.dot(a, wo_ref[...], preferred_element_type=F32)


def _ffn(x, norm_w, w_in, w_out, *, tm=1024, tf=512):
    n, d = x.shape
    nf = D_FF // tf
    est = (2 * 2 * _nbytes((tm, d), F32) + _nbytes((tm, d), BF16)
           + 2 * 3 * _nbytes((d, tf), BF16) + 3 * _nbytes((tm, tf), F32))
    return pl.pallas_call(
        _ffn_kernel,
        grid=(n // tm, nf),
        in_specs=[
            pl.BlockSpec((tm, d), lambda i, f: (i, 0)),
            pl.BlockSpec((1, d), lambda i, f: (0, 0)),
            pl.BlockSpec((d, tf), lambda i, f: (0, f)),
            pl.BlockSpec((d, tf), lambda i, f: (0, f + nf)),
            pl.BlockSpec((tf, d), lambda i, f: (f, 0)),
        ],
        out_specs=pl.BlockSpec((tm, d), lambda i, f: (i, 0)),
        out_shape=jax.ShapeDtypeStruct((n, d), F32),
        scratch_shapes=[pltpu.VMEM((tm, d), BF16)],
        compiler_params=_params(("parallel", "arbitrary"), est),
        name="ffn",
    )(x, norm_w.reshape(1, d), w_in, w_in, w_out)


IN_TN = 512
NB_ATT = 3 * ATT_WIDTH // IN_TN
NB_MQK = 2 * MLSTM_WIDTH // IN_TN
NB_MV = MLSTM_WIDTH // IN_TN
NB_MO = MLSTM_WIDTH // IN_TN
IN_PARTS = 4


def _in_proj_kernel(x_ref, nw_ref, w_ref, wgt_ref, bg_ref,
                    att_ref, att4_ref, att16_ref, mqk_ref, mv_ref, mo_ref, gt_ref, h_ref, zs_ref):
    j = pl.program_id(1)
    tm = x_ref.shape[0]

    @pl.when(j == 0)
    def _():
        h = _rms(x_ref[...], nw_ref[...]).astype(BF16)
        h_ref[...] = h
        gt_ref[...] = lax.dot_general(wgt_ref[...], h, NT_DIMS, preferred_element_type=F32) + bg_ref[...]

    @pl.when(j < NB_ATT)
    def _():
        part = tm // IN_PARTS
        for c in range(IN_PARTS):
            z = jnp.dot(h_ref[c * part:(c + 1) * part, :], w_ref[...], preferred_element_type=F32)
            att_ref[0, 0, c * part:(c + 1) * part, :] = z.astype(BF16)
            for s in range(IN_TN // LANES):
                zs_ref[s, c * part:(c + 1) * part, :] = z[:, s * LANES:(s + 1) * LANES]
            for d, ref in ((DILATIONS[1], att4_ref), (DILATIONS[2], att16_ref)):
                rows = part // d
                for r in range(d):
                    for s in range(IN_TN // LANES):
                        piece = zs_ref[s, pl.ds(c * part + r, rows, stride=d), :]
                        ref[0, r, c * rows:(c + 1) * rows, s * LANES:(s + 1) * LANES] = piece.astype(BF16)

    @pl.when(j >= NB_ATT)
    def _():
        z = jnp.dot(h_ref[...], w_ref[...], preferred_element_type=F32)

        @pl.when(j < NB_ATT + NB_MQK)
        def _():
            mqk_ref[...] = z

        @pl.when((j >= NB_ATT + NB_MQK) & (j < NB_ATT + NB_MQK + NB_MV))
        def _():
            mv_ref[...] = z.astype(BF16)

        @pl.when(j >= NB_ATT + NB_MQK + NB_MV)
        def _():
            mo_ref[...] = z


def _in_proj(x, norm_w, w_main, w_gates_t, b_gates, *, seq, tm=1024):
    n, d = x.shape
    tn = IN_TN
    nb = NB_ATT + NB_MQK + NB_MV + NB_MO
    o1, o2, o3 = NB_ATT, NB_ATT + NB_MQK, NB_ATT + NB_MQK + NB_MV
    batch, tpb = n // seq, seq // tm

    def col(off, cnt):
        return lambda i, j: (i, jnp.clip(j - off, 0, cnt - 1))

    def att_spec(dil):
        return pl.BlockSpec((1, dil, tm // dil, tn),
                            lambda i, j: (i // tpb, 0, i % tpb, jnp.minimum(j, NB_ATT - 1)))

    def att_shape(dil):
        return jax.ShapeDtypeStruct((batch, dil, seq // dil, 3 * ATT_WIDTH), BF16)

    est = (2 * _nbytes((tm, d), F32) + _nbytes((tm, d), BF16) + 2 * _nbytes((d, tn), BF16)
           + 2 * (2 * _nbytes((tm, tn), F32) + 4 * _nbytes((tm, tn), BF16)) + 3 * _nbytes((tm, tn), F32))
    return pl.pallas_call(
        _in_proj_kernel,
        grid=(n // tm, nb),
        in_specs=[
            pl.BlockSpec((tm, d), lambda i, j: (i, 0)),
            pl.BlockSpec((1, d), lambda i, j: (0, 0)),
            pl.BlockSpec((d, tn), lambda i, j: (0, j)),
            pl.BlockSpec((N_GATES, d), lambda i, j: (0, 0)),
            pl.BlockSpec((N_GATES, 1), lambda i, j: (0, 0)),
        ],
        out_specs=[
            att_spec(DILATIONS[0]),
            att_spec(DILATIONS[1]),
            att_spec(DILATIONS[2]),
            pl.BlockSpec((tm, tn), col(o1, NB_MQK)),
            pl.BlockSpec((tm, tn), col(o2, NB_MV)),
            pl.BlockSpec((tm, tn), col(o3, NB_MO)),
            pl.BlockSpec((N_GATES, tm), lambda i, j: (0, i)),
        ],
        out_shape=[
            att_shape(DILATIONS[0]),
            att_shape(DILATIONS[1]),
            att_shape(DILATIONS[2]),
            jax.ShapeDtypeStruct((n, 2 * MLSTM_WIDTH), F32),
            jax.ShapeDtypeStruct((n, MLSTM_WIDTH), BF16),
            jax.ShapeDtypeStruct((n, MLSTM_WIDTH), F32),
            jax.ShapeDtypeStruct((N_GATES, n), F32),
        ],
        scratch_shapes=[pltpu.VMEM((tm, d), BF16), pltpu.VMEM((tn // LANES, tm, LANES), F32)],
        compiler_params=_params(("parallel", "arbitrary"), est),
        name="in_proj",
    )(x, norm_w.reshape(1, d), w_main, w_gates_t, b_gates.reshape(N_GATES, 1))


def _rel_bucket(rel):
    nb = N_BUCKETS // 2
    max_exact = nb // 2
    n = np.abs(rel)
    large = max_exact + (np.log(np.maximum(n, 1) / max_exact) / math.log(MAX_DISTANCE / max_exact)
                         * (nb - max_exact)).astype(np.int32)
    large = np.minimum(large, nb - 1)
    return np.where(rel > 0, nb, 0) + np.where(n < max_exact, n, large)


def _branch_bias(rel_table, dilation):
    key = np.arange(K_BLOCK)[None, :]
    off = key - HALF_WINDOW - np.arange(Q_BLOCK)[:, None]
    band = np.abs(off) <= HALF_WINDOW
    keep = np.stack([band & ((v & 1 == 0) | (key >= HALF_WINDOW)) & ((v & 2 == 0) | (key < K_BLOCK - HALF_WINDOW))
                     for v in range(4)])
    onehot = (_rel_bucket(off * dilation).reshape(1, -1) == np.arange(N_BUCKETS)[:, None]).astype(np.float32)
    bias = jnp.einsum("bh,bn->hn", rel_table.astype(F32) * LOG2_E, jnp.asarray(onehot),
                      precision=lax.Precision.HIGHEST)
    return jnp.where(keep[:, None], bias.reshape(1, ATT_HEADS, Q_BLOCK, K_BLOCK), NEG)


def _attn_branch_kernel(q_ref, k_ref, v_ref, bias_ref, o_ref, st_ref, kpad, vpad, *, seq, width):
    pw = pl.program_id(2)
    pairs = width // LANES

    zeros = jnp.zeros((HALF_WINDOW, width), BF16)
    kpad[0:HALF_WINDOW, :] = zeros
    vpad[0:HALF_WINDOW, :] = zeros
    kpad[HALF_WINDOW + seq:2 * HALF_WINDOW + seq, :] = zeros
    vpad[HALF_WINDOW + seq:2 * HALF_WINDOW + seq, :] = zeros
    kpad[HALF_WINDOW:HALF_WINDOW + seq, :] = k_ref[0, 0]
    vpad[HALF_WINDOW:HALF_WINDOW + seq, :] = v_ref[0, 0]

    @pl.when(pw == 0)
    def _():
        st_ref[...] = jnp.zeros_like(st_ref)

    lane = lax.broadcasted_iota(jnp.int32, (1, LANES), 1)
    low = lane < ATT_HEAD_DIM
    n_blocks = seq // Q_BLOCK
    unroll = math.gcd(n_blocks, ATT_UNROLL)

    for p in range(pairs):
        sl = slice(p * LANES, (p + 1) * LANES)
        head_pair = pw * pairs + p

        def block(q0, sl=sl, p=p, head_pair=head_pair):
            qb = q_ref[0, 0, pl.ds(q0, Q_BLOCK), sl]
            kw = kpad[pl.ds(q0, K_BLOCK), sl]
            vw = vpad[pl.ds(q0, K_BLOCK), sl]
            var = jnp.where(q0 == 0, 1, 0) + jnp.where(q0 == seq - Q_BLOCK, 2, 0)
            zero = jnp.zeros_like(qb)
            qs = jnp.concatenate([jnp.where(low, qb, zero), jnp.where(low, zero, qb)], axis=0)
            s = lax.dot_general(qs, kw, NT_DIMS, preferred_element_type=F32)
            s0 = s[:Q_BLOCK] + bias_ref[var, 2 * p]
            s1 = s[Q_BLOCK:] + bias_ref[var, 2 * p + 1]
            m0 = jnp.max(s0, axis=1, keepdims=True)
            m1 = jnp.max(s1, axis=1, keepdims=True)
            e0 = jnp.exp2(s0 - m0)
            e1 = jnp.exp2(s1 - m1)
            l0 = jnp.sum(e0, axis=1, keepdims=True)
            l1 = jnp.sum(e1, axis=1, keepdims=True)
            e = jnp.concatenate([e0, e1], axis=0).astype(BF16)
            pv = jnp.dot(e, vw, preferred_element_type=F32)
            o_ref[0, 0, pl.ds(q0, Q_BLOCK), sl] = jnp.where(low, pv[:Q_BLOCK], pv[Q_BLOCK:])
            h0 = 2 * head_pair
            prev = st_ref[0, 0, pl.ds(q0, Q_BLOCK), :]
            st_ref[0, 0, pl.ds(q0, Q_BLOCK), :] = jnp.where(
                lane == h0, m0, jnp.where(
                    lane == h0 + 1, m1, jnp.where(
                        lane == ATT_HEADS + h0, l0, jnp.where(lane == ATT_HEADS + h0 + 1, l1, prev))))

        def body(i, carry, block=block):
            for u in range(unroll):
                block(pl.multiple_of((i * unroll + u) * Q_BLOCK, Q_BLOCK))
            return carry

        lax.fori_loop(0, n_blocks // unroll, body, 0)


def _attn_width(seq):
    return int(min(ATT_WIDTH, max(LANES, (1 << 20) // seq // LANES * LANES)))


def _attn_branch(att, bias):
    b, d, seq, _ = att.shape
    width = _attn_width(seq)
    npw = ATT_WIDTH // width
    est = (2 * 3 * _nbytes((seq, width), BF16) + 2 * _nbytes((4, width // ATT_HEAD_DIM, Q_BLOCK, K_BLOCK), F32)
           + 2 * _nbytes((seq, width), F32) + 2 * _nbytes((seq, LANES), F32)
           + 2 * _nbytes((seq + 2 * HALF_WINDOW, width), BF16))
    kern = functools.partial(_attn_branch_kernel, seq=seq, width=width)
    return pl.pallas_call(
        kern,
        grid=(b, d, npw),
        in_specs=[
            pl.BlockSpec((1, 1, seq, width), lambda bi, r, pw: (bi, r, 0, pw)),
            pl.BlockSpec((1, 1, seq, width), lambda bi, r, pw: (bi, r, 0, npw + pw)),
            pl.BlockSpec((1, 1, seq, width), lambda bi, r, pw: (bi, r, 0, 2 * npw + pw)),
            pl.BlockSpec((4, width // ATT_HEAD_DIM, Q_BLOCK, K_BLOCK), lambda bi, r, pw: (0, pw, 0, 0)),
        ],
        out_specs=[
            pl.BlockSpec((1, 1, seq, width), lambda bi, r, pw: (bi, r, 0, pw)),
            pl.BlockSpec((1, 1, seq, LANES), lambda bi, r, pw: (bi, r, 0, 0)),
        ],
        out_shape=[
            jax.ShapeDtypeStruct((b, d, seq, ATT_WIDTH), F32),
            jax.ShapeDtypeStruct((b, d, seq, LANES), F32),
        ],
        scratch_shapes=[pltpu.VMEM((seq + 2 * HALF_WINDOW, width), BF16),
                        pltpu.VMEM((seq + 2 * HALF_WINDOW, width), BF16)],
        compiler_params=_params(("parallel", "parallel", "arbitrary"), est),
        name=f"attn_d{d}",
    )(att, att, att, bias)


def _attn_merge_kernel(n1_ref, n4_ref, n16_ref, s1_ref, s4_ref, s16_ref, ex_ref, nw_ref, y_ref,
                       t4_ref, t16_ref, out_ref):
    tm = y_ref.shape[0]
    lane = lax.broadcasted_iota(jnp.int32, (1, LANES), 1)
    head = lane < ATT_HEADS

    def natural(ref, tmp_ref, s):
        d = ref.shape[1]
        if d == 1:
            return ref[0, 0, :, s * LANES:(s + 1) * LANES]
        for r in range(d):
            tmp_ref[pl.ds(r, tm // d, stride=d), :] = ref[0, r, :, s * LANES:(s + 1) * LANES]
        return tmp_ref[...]

    st = [natural(s1_ref, None, 0), natural(s4_ref, t4_ref, 0), natural(s16_ref, t16_ref, 0)]
    m = jnp.maximum(jnp.maximum(st[0], st[1]), st[2])
    scale = [jnp.exp2(x - m) for x in st]
    den = sum(sc * pltpu.roll(x, LANES - ATT_HEADS, axis=1) for sc, x in zip(scale, st))
    inv = 1.0 / den
    ex = ex_ref[...]

    def spread(w):
        w = jnp.where(head, w, 0.0)
        hi = w.astype(BF16)
        lo = (w - hi.astype(F32)).astype(BF16)
        return (jnp.dot(hi, ex, preferred_element_type=F32) + jnp.dot(lo, ex, preferred_element_type=F32))

    wide = [spread(sc * inv) for sc in scale]
    refs = ((n1_ref, None), (n4_ref, t4_ref), (n16_ref, t16_ref))
    for s in range(ATT_WIDTH // LANES):
        sl = slice(s * LANES, (s + 1) * LANES)
        out_ref[:, sl] = sum(w[:, sl] * natural(ref, tmp, s) for w, (ref, tmp) in zip(wide, refs))
    y_ref[...] = _rms(out_ref[...], nw_ref[...]).astype(BF16)


def _attn_merge(nums, stats, norm_w, *, tm=512):
    b, _, s, _ = nums[0].shape
    n, tpb = b * s, s // tm
    expand = np.zeros((LANES, ATT_WIDTH), np.float32)
    for h in range(ATT_HEADS):
        expand[h, h * ATT_HEAD_DIM:(h + 1) * ATT_HEAD_DIM] = 1.0
    est = (2 * 3 * (_nbytes((tm, ATT_WIDTH), F32) + _nbytes((tm, LANES), F32)) + 5 * _nbytes((tm, ATT_WIDTH), F32)
           + 2 * _nbytes((LANES, ATT_WIDTH), F32))

    def blk(d, c):
        return pl.BlockSpec((1, d, tm // d, c), lambda i: (i // tpb, 0, i % tpb, 0))

    fixed = lambda i: (0, 0)
    return pl.pallas_call(
        _attn_merge_kernel,
        grid=(n // tm,),
        in_specs=[blk(d, ATT_WIDTH) for d in DILATIONS] + [blk(d, LANES) for d in DILATIONS] + [
            pl.BlockSpec((LANES, ATT_WIDTH), fixed), pl.BlockSpec((1, ATT_WIDTH), fixed)],
        out_specs=pl.BlockSpec((tm, ATT_WIDTH), lambda i: (i, 0)),
        out_shape=jax.ShapeDtypeStruct((n, ATT_WIDTH), BF16),
        scratch_shapes=[pltpu.VMEM((tm, LANES), F32), pltpu.VMEM((tm, LANES), F32),
                        pltpu.VMEM((tm, ATT_WIDTH), F32)],
        compiler_params=_params(("parallel",), est),
        name="attn_merge",
    )(*nums, *stats, jnp.asarray(expand, dtype=BF16), norm_w.reshape(1, ATT_WIDTH))


CONV_ROWS = 512
CONV_COLS = 256


def _conv_silu_kernel(x_ref, w_ref, b_ref, y_ref, *, seq):
    cb = pl.program_id(1)
    scale = jnp.where(cb >= MLSTM_WIDTH // CONV_COLS, MLSTM_HEAD_DIM ** -0.5, 1.0).astype(F32)
    w0, w1, w2 = w_ref[0:1, :], w_ref[1:2, :], w_ref[2:3, :]
    bias = b_ref[...]
    rows = lax.broadcasted_iota(jnp.int32, (CONV_ROWS, 1), 0)

    def body(c, carry):
        c0 = pl.multiple_of(c * CONV_ROWS, CONV_ROWS)
        xc = x_ref[0, pl.ds(c0, CONV_ROWS), :]
        before = x_ref[0, pl.ds(pl.multiple_of(jnp.maximum(c0 - 8, 0), 8), 8), :][7:8, :]
        after = x_ref[0, pl.ds(pl.multiple_of(jnp.minimum(c0 + CONV_ROWS, seq - 8), 8), 8), :][0:1, :]
        before = jnp.where(c0 > 0, before, 0.0)
        after = jnp.where(c0 + CONV_ROWS < seq, after, 0.0)
        up = jnp.where(rows == 0, before, pltpu.roll(xc, 1, axis=0))
        dn = jnp.where(rows == CONV_ROWS - 1, after, pltpu.roll(xc, CONV_ROWS - 1, axis=0))
        y = w0 * up + w1 * xc + w2 * dn + bias
        y_ref[0, pl.ds(c0, CONV_ROWS), :] = (y * jax.nn.sigmoid(y) * scale).astype(BF16)
        return carry

    lax.fori_loop(0, seq // CONV_ROWS, body, 0)


def _conv_silu(x, conv_w, conv_b):
    b, s, c = x.shape
    est = 2 * _nbytes((s, CONV_COLS), F32) + 2 * _nbytes((s, CONV_COLS), BF16) + 8 * _nbytes((CONV_ROWS, CONV_COLS), F32)
    return pl.pallas_call(
        functools.partial(_conv_silu_kernel, seq=s),
        grid=(b, c // CONV_COLS),
        in_specs=[
            pl.BlockSpec((1, s, CONV_COLS), lambda bi, cb: (bi, 0, cb)),
            pl.BlockSpec((3, CONV_COLS), lambda bi, cb: (0, cb)),
            pl.BlockSpec((1, CONV_COLS), lambda bi, cb: (0, cb)),
        ],
        out_specs=pl.BlockSpec((1, s, CONV_COLS), lambda bi, cb: (bi, 0, cb)),
        out_shape=jax.ShapeDtypeStruct((b, s, c), BF16),
        compiler_params=_params(("parallel", "parallel"), est),
        name="conv_silu",
    )(x, conv_w, conv_b.reshape(1, c))


EXT = MLSTM_HEAD_DIM + LANES


def _log_sigmoid(x):
    return jnp.minimum(x, 0.0) - jnp.log1p(jnp.exp(-jnp.abs(x)))


def _mlstm_kernel(*refs, reverse, tile):
    if reverse:
        q_ref, k_ref, v_ref, gt_ref, hf_ref, o_ref, nw_ref, out_ref, cext_ref, m_ref = refs
    else:
        q_ref, k_ref, v_ref, gt_ref, out_ref, cext_ref, m_ref = refs
    C = MLSTM_CHUNK
    nchunks = tile // C

    @pl.when(pl.program_id(1) == 0)
    def _():
        cext_ref[...] = jnp.zeros_like(cext_ref)
        m_ref[...] = jnp.full_like(m_ref, NEG)

    r_i = lax.broadcasted_iota(jnp.int32, (C, C), 0)
    c_i = lax.broadcasted_iota(jnp.int32, (C, C), 1)
    if reverse:
        causal, causal_t = c_i >= r_i, r_i >= c_i
    else:
        causal, causal_t = c_i <= r_i, r_i <= c_i
    tri_row = jnp.where(causal_t, 1.0, 0.0).astype(F32)
    tri_col = jnp.where(causal, 1.0, 0.0).astype(F32)
    eye = jnp.where(r_i == c_i, 1.0, 0.0).astype(F32)
    ones_lane = jnp.where(lax.broadcasted_iota(jnp.int32, (C, LANES), 1) == 0, 1.0, 0.0).astype(BF16)
    hi = lax.Precision.HIGHEST
    g_i = 2 * MLSTM_HEADS if reverse else 0
    g_f = g_i + MLSTM_HEADS

    def body(ci, carry):
        c = (nchunks - 1 - ci) if reverse else ci
        t0 = pl.multiple_of(c * C, C)
        gates = gt_ref[:, pl.ds(t0, C)]
        lsig = _log_sigmoid(gates)
        cum_rows = jnp.dot(lsig, tri_row, preferred_element_type=F32, precision=hi)
        cum_cols = lax.dot_general(tri_col, lsig, NT_DIMS, preferred_element_type=F32, precision=hi)
        gate_cols = lax.dot_general(eye, gates, NT_DIMS, preferred_element_type=F32, precision=hi)
        li, li_cols = gates[g_i:g_i + MLSTM_HEADS], gate_cols[:, g_i:g_i + MLSTM_HEADS]
        b_rows, b_cols = cum_rows[g_f:g_f + MLSTM_HEADS], cum_cols[:, g_f:g_f + MLSTM_HEADS]
        for hd in range(MLSTM_HEADS):
            hs = slice(hd * MLSTM_HEAD_DIM, (hd + 1) * MLSTM_HEAD_DIM)
            q = q_ref[0, pl.ds(t0, C), hs]
            k = k_ref[0, pl.ds(t0, C), hs]
            v = v_ref[0, pl.ds(t0, C), hs]
            vext = jnp.concatenate([v, ones_lane], axis=1)
            bc, br = b_cols[:, hd:hd + 1], b_rows[hd:hd + 1, :]
            lir, lic = li[hd:hd + 1, :], li_cols[:, hd:hd + 1]
            m_prev = m_ref[hd, 0:1, 0:1]
            dm = jnp.where(causal, bc - br + lir, -jnp.inf)
            inter = bc + m_prev
            m_t = jnp.maximum(inter, jnp.max(dm, axis=1, keepdims=True))
            sqk = lax.dot_general(q, k, NT_DIMS, preferred_element_type=F32)
            w_intra = (jnp.exp(dm - m_t) * sqk).astype(BF16)
            w_inter = jnp.exp(inter - m_t)
            cext = cext_ref[hd]
            num = (jnp.dot(w_intra, vext, preferred_element_type=F32)
                   + w_inter * jnp.dot(q, cext.astype(BF16), preferred_element_type=F32))
            den = num[:, MLSTM_HEAD_DIM:MLSTM_HEAD_DIM + 1]
            h = num[:, :MLSTM_HEAD_DIM] / jnp.maximum(jnp.abs(den), jnp.exp(-m_t))
            g = br[:, 0:1] if reverse else br[:, C - 1:C]
            m_new = jnp.maximum(g + m_prev, jnp.max(g - br + lir, axis=1, keepdims=True))
            w_a = jnp.exp(g - bc + lic - m_new)
            decay = jnp.exp(g + m_prev - m_new)
            vw = (vext.astype(F32) * w_a).astype(BF16)
            cext_ref[hd] = decay * cext + lax.dot_general(k, vw, TN_DIMS, preferred_element_type=F32)
            m_ref[hd] = jnp.broadcast_to(m_new, m_ref.shape[1:])
            if reverse:
                hsum = h + hf_ref[0, pl.ds(t0, C), hs]
                mu = jnp.mean(hsum, axis=1, keepdims=True)
                dev = hsum - mu
                var = jnp.mean(dev * dev, axis=1, keepdims=True)
                y = dev * lax.rsqrt(var + EPS) * nw_ref[:, hs] * jax.nn.sigmoid(o_ref[0, pl.ds(t0, C), hs])
                out_ref[0, pl.ds(t0, C), hs] = y.astype(BF16)
            else:
                out_ref[0, pl.ds(t0, C), hs] = h
        return carry

    lax.fori_loop(0, nchunks, body, 0)


def _mlstm(qk, v, gates_t, o_pre, norm_w, *, tile=512):
    b, s, _ = v.shape
    ns = s // tile
    w = MLSTM_WIDTH
    scratch = [pltpu.VMEM((MLSTM_HEADS, MLSTM_HEAD_DIM, EXT), F32), pltpu.VMEM((MLSTM_HEADS, 8, LANES), F32)]
    est = (2 * 3 * _nbytes((tile, w), BF16) + 2 * 3 * _nbytes((tile, w), F32)
           + _nbytes((MLSTM_HEADS, MLSTM_HEAD_DIM, EXT), F32) + 16 * _nbytes((MLSTM_CHUNK, EXT), F32))

    def specs(order):
        return [
            pl.BlockSpec((1, tile, w), lambda bi, si: (bi, order(si), 0)),
            pl.BlockSpec((1, tile, w), lambda bi, si: (bi, order(si), 1)),
            pl.BlockSpec((1, tile, w), lambda bi, si: (bi, order(si), 0)),
            pl.BlockSpec((N_GATES, tile), lambda bi, si: (0, bi * ns + order(si))),
        ]

    fwd = lambda si: si
    bwd = lambda si: ns - 1 - si
    h_fwd = pl.pallas_call(
        functools.partial(_mlstm_kernel, reverse=False, tile=tile),
        grid=(b, ns),
        in_specs=specs(fwd),
        out_specs=pl.BlockSpec((1, tile, w), lambda bi, si: (bi, si, 0)),
        out_shape=jax.ShapeDtypeStruct((b, s, w), F32),
        scratch_shapes=scratch,
        compiler_params=_params(("parallel", "arbitrary"), est),
        name="mlstm_fwd",
    )(qk, qk, v, gates_t)
    return pl.pallas_call(
        functools.partial(_mlstm_kernel, reverse=True, tile=tile),
        grid=(b, ns),
        in_specs=specs(bwd) + [
            pl.BlockSpec((1, tile, w), lambda bi, si: (bi, bwd(si), 0)),
            pl.BlockSpec((1, tile, w), lambda bi, si: (bi, bwd(si), 0)),
            pl.BlockSpec((1, w), lambda bi, si: (0, 0)),
        ],
        out_specs=pl.BlockSpec((1, tile, w), lambda bi, si: (bi, bwd(si), 0)),
        out_shape=jax.ShapeDtypeStruct((b, s, w), BF16),
        scratch_shapes=scratch,
        compiler_params=_params(("parallel", "arbitrary"), est),
        name="mlstm_bwd",
    )(qk, qk, v, gates_t, h_fwd, o_pre, norm_w.reshape(1, w))


def _out_proj_kernel(x_ref, ya_ref, ym_ref, wa_ref, wm_ref, o_ref):
    o_ref[...] = (x_ref[...]
                  + jnp.dot(ya_ref[...], wa_ref[...], preferred_element_type=F32)
                  + jnp.dot(ym_ref[...], wm_ref[...], preferred_element_type=F32))


def _out_proj(x, y_att, y_mem, w_out, *, tm=512):
    n, d = x.shape
    est = (2 * 2 * _nbytes((tm, d), F32) + 2 * 2 * _nbytes((tm, ATT_WIDTH), BF16)
           + 2 * _nbytes((d, d), BF16) + 2 * _nbytes((tm, d), F32))
    row = lambda i: (i, 0)
    return pl.pallas_call(
        _out_proj_kernel,
        grid=(n // tm,),
        in_specs=[
            pl.BlockSpec((tm, d), row),
            pl.BlockSpec((tm, ATT_WIDTH), row),
            pl.BlockSpec((tm, MLSTM_WIDTH), row),
            pl.BlockSpec((ATT_WIDTH, d), lambda i: (0, 0)),
            pl.BlockSpec((MLSTM_WIDTH, d), lambda i: (1, 0)),
        ],
        out_specs=pl.BlockSpec((tm, d), row),
        out_shape=jax.ShapeDtypeStruct((n, d), F32),
        compiler_params=_params(("parallel",), est),
        name="out_proj",
    )(x, y_att, y_mem, w_out, w_out)


def _ple_final_kernel(x_ref, p_ref, nw_ref, wg_ref, wp_ref, fw_ref, o_ref):
    x = x_ref[...]
    h = _rms(x, nw_ref[...]).astype(BF16)
    gate = jax.nn.sigmoid(jnp.dot(h, wg_ref[...], preferred_element_type=F32))
    proj = jnp.dot(p_ref[...].astype(BF16), wp_ref[...], preferred_element_type=F32)
    o_ref[...] = _rms(x + gate * proj, fw_ref[...])


def _ple_final(x, p, norm_w, w_gate, w_proj, final_w, *, tm=512):
    n, d = x.shape
    est = (2 * 2 * _nbytes((tm, d), F32) + 2 * _nbytes((tm, PLE_DIM), F32) + 2 * _nbytes((d, d), BF16)
           + 2 * _nbytes((PLE_DIM, d), BF16) + 4 * _nbytes((tm, d), F32))
    row = lambda i: (i, 0)
    fixed = lambda i: (0, 0)
    return pl.pallas_call(
        _ple_final_kernel,
        grid=(n // tm,),
        in_specs=[
            pl.BlockSpec((tm, d), row),
            pl.BlockSpec((tm, PLE_DIM), row),
            pl.BlockSpec((1, d), fixed),
            pl.BlockSpec((d, d), fixed),
            pl.BlockSpec((PLE_DIM, d), fixed),
            pl.BlockSpec((1, d), fixed),
        ],
        out_specs=pl.BlockSpec((tm, d), row),
        out_shape=jax.ShapeDtypeStruct((n, d), F32),
        compiler_params=_params(("parallel",), est),
        name="ple_final",
    )(x, p, norm_w.reshape(1, d), w_gate, w_proj, final_w.reshape(1, d))


def _trunk(x, p, wts):
    b, s, d = x.shape
    n = b * s
    h = _ffn(x.reshape(n, d), wts["ffn1_norm"], wts["ffn1_w_in"], wts["ffn1_w_out"])
    *atts, mqk, mv, mo, gates_t = _in_proj(h, wts["mix_norm"], wts["w_in_main"], wts["w_in_gates_t"],
                                           wts["b_gates"], seq=s)
    nums, stats = zip(*[_attn_branch(att, wts["bias"][d_]) for att, d_ in zip(atts, DILATIONS)])
    y_att = _attn_merge(nums, stats, wts["attn_out_norm"])
    qk = _conv_silu(mqk.reshape(b, s, 2 * MLSTM_WIDTH), wts["conv_w"], wts["conv_b"])
    y_mem = _mlstm(qk, mv.reshape(b, s, MLSTM_WIDTH), gates_t, mo.reshape(b, s, MLSTM_WIDTH),
                   wts["mlstm_out_norm"])
    h = _out_proj(h, y_att, y_mem.reshape(n, MLSTM_WIDTH), wts["w_out"])
    h = _ffn(h, wts["ffn2_norm"], wts["ffn2_w_in"], wts["ffn2_w_out"])
    y = _ple_final(h, p.reshape(n, PLE_DIM), wts["ple_norm"], wts["ple_w_gate"], wts["ple_w_proj"],
                   wts["final_norm"])
    return y.reshape(b, s, d)


def kernel(x_prompt, x_sample, p_prompt, p_sample, rel_table, ffn1_norm, ffn1_w_in, ffn1_w_out, mix_norm, w_in, b_gates, conv_w, conv_b, attn_out_norm, mlstm_out_norm, w_out, ffn2_norm, ffn2_w_in, ffn2_w_out, ple_norm, ple_w_gate, ple_w_proj, final_norm):
    depth = ffn1_norm.shape[0]
    assert depth == 1
    i = 0
    n_main = 3 * ATT_WIDTH + 4 * MLSTM_WIDTH
    col_scale = jnp.concatenate([jnp.full((ATT_WIDTH,), ATT_HEAD_DIM ** -0.5 * LOG2_E, F32),
                                 jnp.ones((n_main - ATT_WIDTH,), F32)])
    wts = dict(
        ffn1_norm=ffn1_norm[i], ffn1_w_in=ffn1_w_in[i].astype(BF16), ffn1_w_out=ffn1_w_out[i].astype(BF16),
        mix_norm=mix_norm[i],
        w_in_main=(w_in[i][:, :n_main] * col_scale).astype(BF16),
        w_in_gates_t=w_in[i][:, n_main:].T.astype(BF16),
        b_gates=b_gates[i], conv_w=conv_w[i], conv_b=conv_b[i],
        attn_out_norm=attn_out_norm[i], mlstm_out_norm=mlstm_out_norm[i],
        w_out=w_out[i].astype(BF16),
        ffn2_norm=ffn2_norm[i], ffn2_w_in=ffn2_w_in[i].astype(BF16), ffn2_w_out=ffn2_w_out[i].astype(BF16),
        ple_norm=ple_norm[i], ple_w_gate=ple_w_gate[i].astype(BF16), ple_w_proj=ple_w_proj[i].astype(BF16),
        final_norm=final_norm,
        bias={d_: _branch_bias(rel_table, d_) for d_ in DILATIONS},
    )
    return (_trunk(x_prompt, p_prompt[i], wts), _trunk(x_sample, p_sample[i], wts))
```

```python
import functools
import math

import jax
import jax.numpy as jnp
import numpy as np
from jax import lax
from jax.experimental import pallas as pl
from jax.experimental.pallas import tpu as pltpu

F32 = jnp.float32
BF16 = jnp.bfloat16

D_MODEL = 2048
ATT_HEADS = 16
ATT_HEAD_DIM = 64
ATT_WIDTH = ATT_HEADS * ATT_HEAD_DIM
DILATIONS = (1, 4, 16)
HALF_WINDOW = 64
N_BUCKETS = 32
MAX_DISTANCE = 1024
MLSTM_HEADS = 4
MLSTM_HEAD_DIM = 256
MLSTM_WIDTH = MLSTM_HEADS * MLSTM_HEAD_DIM
N_GATES = 4 * MLSTM_HEADS
D_FF = 5632
PLE_DIM = 256
EPS = 1e-6
NEG = -1e30
LOG2_E = math.log2(math.e)

LANES = 128
V7X_VMEM_BYTES = 64 * 1024 * 1024
VMEM_CAP = V7X_VMEM_BYTES - 6 * 1024 * 1024
VMEM_SLACK = 8 * 1024 * 1024

Q_BLOCK = 128
K_BLOCK = Q_BLOCK + 2 * HALF_WINDOW
ATT_UNROLL = 8
MLSTM_CHUNK = {False: 256, True: 128}
NORM_ROWS = 64
MLSTM_UNROLL = 1

NT_DIMS = (((1,), (1,)), ((), ()))
TN_DIMS = (((0,), (0,)), ((), ()))


def _params(semantics, est_bytes):
    limit = int(min(est_bytes + VMEM_SLACK, VMEM_CAP))
    return pltpu.CompilerParams(dimension_semantics=semantics, vmem_limit_bytes=limit)


def _nbytes(shape, dtype):
    return math.prod(shape) * jnp.dtype(dtype).itemsize


def _rms(x, w):
    ms = jnp.mean(x * x, axis=-1, keepdims=True)
    return x * lax.rsqrt(ms + EPS) * w


def _ffn_kernel(x_ref, nw_ref, wg_ref, wu_ref, wo_ref, o_ref, h_ref):
    f = pl.program_id(1)

    @pl.when(f == 0)
    def _():
        x = x_ref[...]
        h_ref[...] = _rms(x, nw_ref[...]).astype(BF16)
        o_ref[...] = x

    h = h_ref[...]
    g = jnp.dot(h, wg_ref[...], preferred_element_type=F32)
    u = jnp.dot(h, wu_ref[...], preferred_element_type=F32)
    a = (g * jax.nn.sigmoid(g) * u * 0.5).astype(BF16)
    o_ref[...] += jnp.dot(a, wo_ref[...], preferred_element_type=F32)


def _ffn(x, norm_w, w_in, w_out, *, tm=1024, tf=512):
    n, d = x.shape
    nf = D_FF // tf
    est = (2 * 2 * _nbytes((tm, d), F32) + _nbytes((tm, d), BF16)
           + 2 * 3 * _nbytes((d, tf), BF16) + 3 * _nbytes((tm, tf), F32))
    return pl.pallas_call(
        _ffn_kernel,
        grid=(n // tm, nf),
        in_specs=[
            pl.BlockSpec((tm, d), lambda i, f: (i, 0)),
            pl.BlockSpec((1, d), lambda i, f: (0, 0)),
            pl.BlockSpec((d, tf), lambda i, f: (0, f)),
            pl.BlockSpec((d, tf), lambda i, f: (0, f + nf)),
            pl.BlockSpec((tf, d), lambda i, f: (f, 0)),
        ],
        out_specs=pl.BlockSpec((tm, d), lambda i, f: (i, 0)),
        out_shape=jax.ShapeDtypeStruct((n, d), F32),
        scratch_shapes=[pltpu.VMEM((tm, d), BF16)],
        compiler_params=_params(("parallel", "arbitrary"), est),
        name="ffn",
    )(x, norm_w.reshape(1, d), w_in, w_in, w_out)


IN_TN = 512
NB_ATT = 3 * ATT_WIDTH // IN_TN
NB_MQK = 2 * MLSTM_WIDTH // IN_TN
NB_MV = MLSTM_WIDTH // IN_TN
NB_MO = MLSTM_WIDTH // IN_TN
IN_PARTS = 4


def _in_proj_kernel(x_ref, nw_ref, w_ref, wgt_ref, bg_ref,
                    att_ref, att4_ref, att16_ref, mqk_ref, mv_ref, mo_ref, gt_ref, h_ref, zs_ref):
    j = pl.program_id(1)
    tm = x_ref.shape[0]

    @pl.when(j == 0)
    def _():
        h = _rms(x_ref[...], nw_ref[...]).astype(BF16)
        h_ref[...] = h
        gt_ref[...] = lax.dot_general(wgt_ref[...], h, NT_DIMS, preferred_element_type=F32) + bg_ref[...]

    @pl.when(j < NB_ATT)
    def _():
        part = tm // IN_PARTS
        for c in range(IN_PARTS):
            z = jnp.dot(h_ref[c * part:(c + 1) * part, :], w_ref[...], preferred_element_type=F32)
            att_ref[0, 0, c * part:(c + 1) * part, :] = z.astype(BF16)
            for s in range(IN_TN // LANES):
                zs_ref[s, c * part:(c + 1) * part, :] = z[:, s * LANES:(s + 1) * LANES]
            for d, ref in ((DILATIONS[1], att4_ref), (DILATIONS[2], att16_ref)):
                rows = part // d
                for r in range(d):
                    for s in range(IN_TN // LANES):
                        piece = zs_ref[s, pl.ds(c * part + r, rows, stride=d), :]
                        ref[0, r, c * rows:(c + 1) * rows, s * LANES:(s + 1) * LANES] = piece.astype(BF16)

    @pl.when(j >= NB_ATT)
    def _():
        z = jnp.dot(h_ref[...], w_ref[...], preferred_element_type=F32)

        @pl.when(j < NB_ATT + NB_MQK)
        def _():
            mqk_ref[...] = z

        @pl.when((j >= NB_ATT + NB_MQK) & (j < NB_ATT + NB_MQK + NB_MV))
        def _():
            mv_ref[...] = z.astype(BF16)

        @pl.when(j >= NB_ATT + NB_MQK + NB_MV)
        def _():
            mo_ref[...] = z


def _in_proj(x, norm_w, w_main, w_gates_t, b_gates, *, seq, tm=1024):
    n, d = x.shape
    tn = IN_TN
    nb = NB_ATT + NB_MQK + NB_MV + NB_MO
    o1, o2, o3 = NB_ATT, NB_ATT + NB_MQK, NB_ATT + NB_MQK + NB_MV
    batch, tpb = n // seq, seq // tm

    def col(off, cnt):
        return lambda i, j: (i, jnp.clip(j - off, 0, cnt - 1))

    def att_spec(dil):
        return pl.BlockSpec((1, dil, tm // dil, tn),
                            lambda i, j: (i // tpb, 0, i % tpb, jnp.minimum(j, NB_ATT - 1)))

    def att_shape(dil):
        return jax.ShapeDtypeStruct((batch, dil, seq // dil, 3 * ATT_WIDTH), BF16)

    est = (2 * _nbytes((tm, d), F32) + _nbytes((tm, d), BF16) + 2 * _nbytes((d, tn), BF16)
           + 2 * (2 * _nbytes((tm, tn), F32) + 4 * _nbytes((tm, tn), BF16)) + 3 * _nbytes((tm, tn), F32))
    return pl.pallas_call(
        _in_proj_kernel,
        grid=(n // tm, nb),
        in_specs=[
            pl.BlockSpec((tm, d), lambda i, j: (i, 0)),
            pl.BlockSpec((1, d), lambda i, j: (0, 0)),
            pl.BlockSpec((d, tn), lambda i, j: (0, j)),
            pl.BlockSpec((N_GATES, d), lambda i, j: (0, 0)),
            pl.BlockSpec((N_GATES, 1), lambda i, j: (0, 0)),
        ],
        out_specs=[
            att_spec(DILATIONS[0]),
            att_spec(DILATIONS[1]),
            att_spec(DILATIONS[2]),
            pl.BlockSpec((tm, tn), col(o1, NB_MQK)),
            pl.BlockSpec((tm, tn), col(o2, NB_MV)),
            pl.BlockSpec((tm, tn), col(o3, NB_MO)),
            pl.BlockSpec((N_GATES, tm), lambda i, j: (0, i)),
        ],
        out_shape=[
            att_shape(DILATIONS[0]),
            att_shape(DILATIONS[1]),
            att_shape(DILATIONS[2]),
            jax.ShapeDtypeStruct((n, 2 * MLSTM_WIDTH), F32),
            jax.ShapeDtypeStruct((n, MLSTM_WIDTH), BF16),
            jax.ShapeDtypeStruct((n, MLSTM_WIDTH), F32),
            jax.ShapeDtypeStruct((N_GATES, n), F32),
        ],
        scratch_shapes=[pltpu.VMEM((tm, d), BF16), pltpu.VMEM((tn // LANES, tm, LANES), F32)],
        compiler_params=_params(("parallel", "arbitrary"), est),
        name="in_proj",
    )(x, norm_w.reshape(1, d), w_main, w_gates_t, b_gates.reshape(N_GATES, 1))


def _rel_bucket(rel):
    nb = N_BUCKETS // 2
    max_exact = nb // 2
    n = np.abs(rel)
    large = max_exact + (np.log(np.maximum(n, 1) / max_exact) / math.log(MAX_DISTANCE / max_exact)
                         * (nb - max_exact)).astype(np.int32)
    large = np.minimum(large, nb - 1)
    return np.where(rel > 0, nb, 0) + np.where(n < max_exact, n, large)


def _branch_bias(rel_table, dilation):
    key = np.arange(K_BLOCK)[None, :]
    off = key - HALF_WINDOW - np.arange(Q_BLOCK)[:, None]
    band = np.abs(off) <= HALF_WINDOW
    keep = np.stack([band & ((v & 1 == 0) | (key >= HALF_WINDOW)) & ((v & 2 == 0) | (key < K_BLOCK - HALF_WINDOW))
                     for v in range(4)])
    onehot = (_rel_bucket(off * dilation).reshape(1, -1) == np.arange(N_BUCKETS)[:, None]).astype(np.float32)
    bias = jnp.einsum("bh,bn->hn", rel_table.astype(F32) * LOG2_E, jnp.asarray(onehot),
                      precision=lax.Precision.HIGHEST)
    return jnp.where(keep[:, None], bias.reshape(1, ATT_HEADS, Q_BLOCK, K_BLOCK), NEG)


def _attn_branch_kernel(q_ref, k_ref, v_ref, bias_ref, o_ref, st_ref, kpad, vpad, *, seq, width):
    pw = pl.program_id(2)
    pairs = width // LANES

    zeros = jnp.zeros((HALF_WINDOW, width), BF16)
    kpad[0:HALF_WINDOW, :] = zeros
    vpad[0:HALF_WINDOW, :] = zeros
    kpad[HALF_WINDOW + seq:2 * HALF_WINDOW + seq, :] = zeros
    vpad[HALF_WINDOW + seq:2 * HALF_WINDOW + seq, :] = zeros
    kpad[HALF_WINDOW:HALF_WINDOW + seq, :] = k_ref[0, 0]
    vpad[HALF_WINDOW:HALF_WINDOW + seq, :] = v_ref[0, 0]

    @pl.when(pw == 0)
    def _():
        st_ref[...] = jnp.zeros_like(st_ref)

    lane = lax.broadcasted_iota(jnp.int32, (1, LANES), 1)
    low = lane < ATT_HEAD_DIM
    n_blocks = seq // Q_BLOCK
    unroll = math.gcd(n_blocks, ATT_UNROLL)

    for p in range(pairs):
        sl = slice(p * LANES, (p + 1) * LANES)
        head_pair = pw * pairs + p

        def block(q0, sl=sl, p=p, head_pair=head_pair):
            qb = q_ref[0, 0, pl.ds(q0, Q_BLOCK), sl]
            kw = kpad[pl.ds(q0, K_BLOCK), sl]
            vw = vpad[pl.ds(q0, K_BLOCK), sl]
            var = jnp.where(q0 == 0, 1, 0) + jnp.where(q0 == seq - Q_BLOCK, 2, 0)
            zero = jnp.zeros_like(qb)
            qs = jnp.concatenate([jnp.where(low, qb, zero), jnp.where(low, zero, qb)], axis=0)
            s = lax.dot_general(qs, kw, NT_DIMS, preferred_element_type=F32)
            s0 = s[:Q_BLOCK] + bias_ref[var, 2 * p]
            s1 = s[Q_BLOCK:] + bias_ref[var, 2 * p + 1]
            m0 = jnp.max(s0, axis=1, keepdims=True)
            m1 = jnp.max(s1, axis=1, keepdims=True)
            e0 = jnp.exp2(s0 - m0)
            e1 = jnp.exp2(s1 - m1)
            l0 = jnp.sum(e0, axis=1, keepdims=True)
            l1 = jnp.sum(e1, axis=1, keepdims=True)
            e = jnp.concatenate([e0, e1], axis=0).astype(BF16)
            pv = jnp.dot(e, vw, preferred_element_type=F32)
            o_ref[0, 0, pl.ds(q0, Q_BLOCK), sl] = jnp.where(low, pv[:Q_BLOCK], pv[Q_BLOCK:])
            h0 = 2 * head_pair
            prev = st_ref[0, 0, pl.ds(q0, Q_BLOCK), :]
            st_ref[0, 0, pl.ds(q0, Q_BLOCK), :] = jnp.where(
                lane == h0, m0, jnp.where(
                    lane == h0 + 1, m1, jnp.where(
                        lane == ATT_HEADS + h0, l0, jnp.where(lane == ATT_HEADS + h0 + 1, l1, prev))))

        def body(i, carry, block=block):
            for u in range(unroll):
                block(pl.multiple_of((i * unroll + u) * Q_BLOCK, Q_BLOCK))
            return carry

        lax.fori_loop(0, n_blocks // unroll, body, 0)


def _attn_width(seq):
    return int(min(ATT_WIDTH, max(LANES, (1 << 20) // seq // LANES * LANES)))


def _attn_branch(att, bias):
    b, d, seq, _ = att.shape
    width = _attn_width(seq)
    npw = ATT_WIDTH // width
    est = (2 * 3 * _nbytes((seq, width), BF16) + 2 * _nbytes((4, width // ATT_HEAD_DIM, Q_BLOCK, K_BLOCK), F32)
           + 2 * _nbytes((seq, width), F32) + 2 * _nbytes((seq, LANES), F32)
           + 2 * _nbytes((seq + 2 * HALF_WINDOW, width), BF16))
    kern = functools.partial(_attn_branch_kernel, seq=seq, width=width)
    return pl.pallas_call(
        kern,
        grid=(b, d, npw),
        in_specs=[
            pl.BlockSpec((1, 1, seq, width), lambda bi, r, pw: (bi, r, 0, pw)),
            pl.BlockSpec((1, 1, seq, width), lambda bi, r, pw: (bi, r, 0, npw + pw)),
            pl.BlockSpec((1, 1, seq, width), lambda bi, r, pw: (bi, r, 0, 2 * npw + pw)),
            pl.BlockSpec((4, width // ATT_HEAD_DIM, Q_BLOCK, K_BLOCK), lambda bi, r, pw: (0, pw, 0, 0)),
        ],
        out_specs=[
            pl.BlockSpec((1, 1, seq, width), lambda bi, r, pw: (bi, r, 0, pw)),
            pl.BlockSpec((1, 1, seq, LANES), lambda bi, r, pw: (bi, r, 0, 0)),
        ],
        out_shape=[
            jax.ShapeDtypeStruct((b, d, seq, ATT_WIDTH), F32),
            jax.ShapeDtypeStruct((b, d, seq, LANES), F32),
        ],
        scratch_shapes=[pltpu.VMEM((seq + 2 * HALF_WINDOW, width), BF16),
                        pltpu.VMEM((seq + 2 * HALF_WINDOW, width), BF16)],
        compiler_params=_params(("parallel", "parallel", "arbitrary"), est),
        name=f"attn_d{d}",
    )(att, att, att, bias)


def _attn_merge_kernel(n1_ref, n4_ref, n16_ref, s1_ref, s4_ref, s16_ref, ex_ref, nw_ref, y_ref,
                       t4_ref, t16_ref, out_ref):
    tm = y_ref.shape[0]
    lane = lax.broadcasted_iota(jnp.int32, (1, LANES), 1)
    head = lane < ATT_HEADS

    def natural(ref, tmp_ref, s):
        d = ref.shape[1]
        if d == 1:
            return ref[0, 0, :, s * LANES:(s + 1) * LANES]
        for r in range(d):
            tmp_ref[pl.ds(r, tm // d, stride=d), :] = ref[0, r, :, s * LANES:(s + 1) * LANES]
        return tmp_ref[...]

    st = [natural(s1_ref, None, 0), natural(s4_ref, t4_ref, 0), natural(s16_ref, t16_ref, 0)]
    m = jnp.maximum(jnp.maximum(st[0], st[1]), st[2])
    scale = [jnp.exp2(x - m) for x in st]
    den = sum(sc * pltpu.roll(x, LANES - ATT_HEADS, axis=1) for sc, x in zip(scale, st))
    inv = 1.0 / den
    ex = ex_ref[...]

    def spread(w):
        w = jnp.where(head, w, 0.0)
        hi = w.astype(BF16)
        lo = (w - hi.astype(F32)).astype(BF16)
        return (jnp.dot(hi, ex, preferred_element_type=F32) + jnp.dot(lo, ex, preferred_element_type=F32))

    wide = [spread(sc * inv) for sc in scale]
    refs = ((n1_ref, None), (n4_ref, t4_ref), (n16_ref, t16_ref))
    for s in range(ATT_WIDTH // LANES):
        sl = slice(s * LANES, (s + 1) * LANES)
        out_ref[:, sl] = sum(w[:, sl] * natural(ref, tmp, s) for w, (ref, tmp) in zip(wide, refs))
    y_ref[...] = _rms(out_ref[...], nw_ref[...]).astype(BF16)


def _attn_merge(nums, stats, norm_w, *, tm=512):
    b, _, s, _ = nums[0].shape
    n, tpb = b * s, s // tm
    expand = np.zeros((LANES, ATT_WIDTH), np.float32)
    for h in range(ATT_HEADS):
        expand[h, h * ATT_HEAD_DIM:(h + 1) * ATT_HEAD_DIM] = 1.0
    est = (2 * 3 * (_nbytes((tm, ATT_WIDTH), F32) + _nbytes((tm, LANES), F32)) + 5 * _nbytes((tm, ATT_WIDTH), F32)
           + 2 * _nbytes((LANES, ATT_WIDTH), F32))

    def blk(d, c):
        return pl.BlockSpec((1, d, tm // d, c), lambda i: (i // tpb, 0, i % tpb, 0))

    fixed = lambda i: (0, 0)
    return pl.pallas_call(
        _attn_merge_kernel,
        grid=(n // tm,),
        in_specs=[blk(d, ATT_WIDTH) for d in DILATIONS] + [blk(d, LANES) for d in DILATIONS] + [
            pl.BlockSpec((LANES, ATT_WIDTH), fixed), pl.BlockSpec((1, ATT_WIDTH), fixed)],
        out_specs=pl.BlockSpec((tm, ATT_WIDTH), lambda i: (i, 0)),
        out_shape=jax.ShapeDtypeStruct((n, ATT_WIDTH), BF16),
        scratch_shapes=[pltpu.VMEM((tm, LANES), F32), pltpu.VMEM((tm, LANES), F32),
                        pltpu.VMEM((tm, ATT_WIDTH), F32)],
        compiler_params=_params(("parallel",), est),
        name="attn_merge",
    )(*nums, *stats, jnp.asarray(expand, dtype=BF16), norm_w.reshape(1, ATT_WIDTH))


CONV_ROWS = 512
CONV_COLS = 256


def _conv_silu_kernel(x_ref, w_ref, b_ref, y_ref, *, seq):
    cb = pl.program_id(1)
    scale = jnp.where(cb >= MLSTM_WIDTH // CONV_COLS, MLSTM_HEAD_DIM ** -0.5, 1.0).astype(F32)
    w0, w1, w2 = w_ref[0:1, :], w_ref[1:2, :], w_ref[2:3, :]
    bias = b_ref[...]
    rows = lax.broadcasted_iota(jnp.int32, (CONV_ROWS, 1), 0)

    def body(c, carry):
        c0 = pl.multiple_of(c * CONV_ROWS, CONV_ROWS)
        xc = x_ref[0, pl.ds(c0, CONV_ROWS), :]
        before = x_ref[0, pl.ds(pl.multiple_of(jnp.maximum(c0 - 8, 0), 8), 8), :][7:8, :]
        after = x_ref[0, pl.ds(pl.multiple_of(jnp.minimum(c0 + CONV_ROWS, seq - 8), 8), 8), :][0:1, :]
        before = jnp.where(c0 > 0, before, 0.0)
        after = jnp.where(c0 + CONV_ROWS < seq, after, 0.0)
        up = jnp.where(rows == 0, before, pltpu.roll(xc, 1, axis=0))
        dn = jnp.where(rows == CONV_ROWS - 1, after, pltpu.roll(xc, CONV_ROWS - 1, axis=0))
        y = w0 * up + w1 * xc + w2 * dn + bias
        y_ref[0, pl.ds(c0, CONV_ROWS), :] = (y * jax.nn.sigmoid(y) * scale).astype(BF16)
        return carry

    lax.fori_loop(0, seq // CONV_ROWS, body, 0)


def _conv_silu(x, conv_w, conv_b):
    b, s, c = x.shape
    est = 2 * _nbytes((s, CONV_COLS), F32) + 2 * _nbytes((s, CONV_COLS), BF16) + 8 * _nbytes((CONV_ROWS, CONV_COLS), F32)
    return pl.pallas_call(
        functools.partial(_conv_silu_kernel, seq=s),
        grid=(b, c // CONV_COLS),
        in_specs=[
            pl.BlockSpec((1, s, CONV_COLS), lambda bi, cb: (bi, 0, cb)),
            pl.BlockSpec((3, CONV_COLS), lambda bi, cb: (0, cb)),
            pl.BlockSpec((1, CONV_COLS), lambda bi, cb: (0, cb)),
        ],
        out_specs=pl.BlockSpec((1, s, CONV_COLS), lambda bi, cb: (bi, 0, cb)),
        out_shape=jax.ShapeDtypeStruct((b, s, c), BF16),
        compiler_params=_params(("parallel", "parallel"), est),
        name="conv_silu",
    )(x, conv_w, conv_b.reshape(1, c))


EXT = MLSTM_HEAD_DIM + LANES


def _log_sigmoid(x):
    return jnp.minimum(x, 0.0) - jnp.log1p(jnp.exp(-jnp.abs(x)))


def _mlstm_kernel(*refs, reverse, tile):
    if reverse:
        q_ref, k_ref, v_ref, gt_ref, hf_ref, o_ref, nw_ref, out_ref, cext_ref, m_ref = refs
    else:
        q_ref, k_ref, v_ref, gt_ref, out_ref, cext_ref, m_ref = refs
    C = MLSTM_CHUNK[reverse]
    nchunks = tile // C

    @pl.when(pl.program_id(1) == 0)
    def _():
        cext_ref[...] = jnp.zeros_like(cext_ref)
        m_ref[...] = jnp.full_like(m_ref, NEG)

    r_i = lax.broadcasted_iota(jnp.int32, (C, C), 0)
    c_i = lax.broadcasted_iota(jnp.int32, (C, C), 1)
    if reverse:
        causal, causal_t = c_i >= r_i, r_i >= c_i
    else:
        causal, causal_t = c_i <= r_i, r_i <= c_i
    tri_row = jnp.where(causal_t, 1.0, 0.0).astype(F32)
    tri_col = jnp.where(causal, 1.0, 0.0).astype(F32)
    eye = jnp.where(r_i == c_i, 1.0, 0.0).astype(F32)
    ones_lane = jnp.ones((C, LANES), BF16)
    hi = lax.Precision.HIGHEST
    g_i = 2 * MLSTM_HEADS if reverse else 0
    g_f = g_i + MLSTM_HEADS

    def chunk(ci):
        c = (nchunks - 1 - ci) if reverse else ci
        t0 = pl.multiple_of(c * C, C)
        gates = gt_ref[:, pl.ds(t0, C)]
        lsig = _log_sigmoid(gates)
        cum_rows = jnp.dot(lsig, tri_row, preferred_element_type=F32, precision=hi)
        cum_cols = lax.dot_general(tri_col, lsig, NT_DIMS, preferred_element_type=F32, precision=hi)
        gate_cols = lax.dot_general(eye, gates, NT_DIMS, preferred_element_type=F32, precision=hi)
        li, li_cols = gates[g_i:g_i + MLSTM_HEADS], gate_cols[:, g_i:g_i + MLSTM_HEADS]
        b_rows, b_cols = cum_rows[g_f:g_f + MLSTM_HEADS], cum_cols[:, g_f:g_f + MLSTM_HEADS]
        for hd in range(MLSTM_HEADS):
            hs = slice(hd * MLSTM_HEAD_DIM, (hd + 1) * MLSTM_HEAD_DIM)
            q = q_ref[0, pl.ds(t0, C), hs]
            k = k_ref[0, pl.ds(t0, C), hs]
            v = v_ref[0, pl.ds(t0, C), hs]
            vext = jnp.concatenate([v, ones_lane], axis=1)
            bc, br = b_cols[:, hd:hd + 1], b_rows[hd:hd + 1, :]
            lir, lic = li[hd:hd + 1, :], li_cols[:, hd:hd + 1]
            m_prev = m_ref[hd, 0:1, 0:1]
            dm = jnp.where(causal, bc - br + lir, -jnp.inf)
            inter = bc + m_prev
            m_t = jnp.maximum(inter, jnp.max(dm, axis=1, keepdims=True))
            sqk = lax.dot_general(q, k, NT_DIMS, preferred_element_type=F32)
            w_intra = (jnp.exp(dm - m_t) * sqk).astype(BF16)
            w_inter = jnp.exp(inter - m_t)
            cext = cext_ref[hd]
            num = (jnp.dot(w_intra, vext, preferred_element_type=F32)
                   + w_inter * jnp.dot(q, cext.astype(BF16), preferred_element_type=F32))
            den =jnp.concatenate([num[:, MLSTM_HEAD_DIM:]] * (MLSTM_HEAD_DIM // LANES), axis=1)
            h = num[:, :MLSTM_HEAD_DIM] / jnp.maximum(jnp.abs(den), jnp.exp(-m_t))
            g = br[:, 0:1] if reverse else br[:, C - 1:C]
            m_new = jnp.maximum(g + m_prev, jnp.max(g - br + lir, axis=1, keepdims=True))
            w_a = jnp.exp(g - bc + lic - m_new)
            decay = jnp.exp(g + m_prev - m_new)
            vw = (vext.astype(F32) * w_a).astype(BF16)
            cext_ref[hd] = decay * cext + lax.dot_general(k, vw, TN_DIMS, preferred_element_type=F32)
            m_ref[hd] = jnp.broadcast_to(m_new, m_ref.shape[1:])
            if reverse:
                for r0 in range(0, C, NORM_ROWS):
                    rows = pl.ds(t0 + r0, NORM_ROWS)
                    hsum = h[r0:r0 + NORM_ROWS] + hf_ref[0, rows, hs]
                    mu = jnp.mean(hsum, axis=1, keepdims=True)
                    dev = hsum - mu
                    var = jnp.mean(dev * dev, axis=1, keepdims=True)
                    y = dev * lax.rsqrt(var + EPS) * nw_ref[:, hs] * jax.nn.sigmoid(o_ref[0, rows, hs])
                    out_ref[0, rows, hs] = y.astype(BF16)
            else:
                out_ref[0, pl.ds(t0, C), hs] = h

    def body(i, carry):
        for u in range(MLSTM_UNROLL):
            chunk(i * MLSTM_UNROLL + u)
        return carry

    lax.fori_loop(0, nchunks // MLSTM_UNROLL, body, 0)


def _mlstm(qk, v, gates_t, o_pre, norm_w, *, tile=512):
    b, s, _ = v.shape
    ns = s // tile
    w = MLSTM_WIDTH
    scratch = [pltpu.VMEM((MLSTM_HEADS, MLSTM_HEAD_DIM, EXT), F32), pltpu.VMEM((MLSTM_HEADS, 8, LANES), F32)]
    est = (2 * 3 * _nbytes((tile, w), BF16) + 2 * 3 * _nbytes((tile, w), F32)
           + _nbytes((MLSTM_HEADS, MLSTM_HEAD_DIM, EXT), F32) + 16 * _nbytes((max(MLSTM_CHUNK.values()), EXT), F32))

    def specs(order):
        return [
            pl.BlockSpec((1, tile, w), lambda bi, si: (bi, order(si), 0)),
            pl.BlockSpec((1, tile, w), lambda bi, si: (bi, order(si), 1)),
            pl.BlockSpec((1, tile, w), lambda bi, si: (bi, order(si), 0)),
            pl.BlockSpec((N_GATES, tile), lambda bi, si: (0, bi * ns + order(si))),
        ]

    fwd = lambda si: si
    bwd = lambda si: ns - 1 - si
    h_fwd = pl.pallas_call(
        functools.partial(_mlstm_kernel, reverse=False, tile=tile),
        grid=(b, ns),
        in_specs=specs(fwd),
        out_specs=pl.BlockSpec((1, tile, w), lambda bi, si: (bi, si, 0)),
        out_shape=jax.ShapeDtypeStruct((b, s, w), F32),
        scratch_shapes=scratch,
        compiler_params=_params(("parallel", "arbitrary"), est),
        name="mlstm_fwd",
    )(qk, qk, v, gates_t)
    return pl.pallas_call(
        functools.partial(_mlstm_kernel, reverse=True, tile=tile),
        grid=(b, ns),
        in_specs=specs(bwd) + [
            pl.BlockSpec((1, tile, w), lambda bi, si: (bi, bwd(si), 0)),
            pl.BlockSpec((1, tile, w), lambda bi, si: (bi, bwd(si), 0)),
            pl.BlockSpec((1, w), lambda bi, si: (0, 0)),
        ],
        out_specs=pl.BlockSpec((1, tile, w), lambda bi, si: (bi, bwd(si), 0)),
        out_shape=jax.ShapeDtypeStruct((b, s, w), BF16),
        scratch_shapes=scratch,
        compiler_params=_params(("parallel", "arbitrary"), est),
        name="mlstm_bwd",
    )(qk, qk, v, gates_t, h_fwd, o_pre, norm_w.reshape(1, w))


def _out_proj_kernel(x_ref, ya_ref, ym_ref, wa_ref, wm_ref, o_ref):
    o_ref[...] = (x_ref[...]
                  + jnp.dot(ya_ref[...], wa_ref[...], preferred_element_type=F32)
                  + jnp.dot(ym_ref[...], wm_ref[...], preferred_element_type=F32))


def _out_proj(x, y_att, y_mem, w_out, *, tm=512):
    n, d = x.shape
    est = (2 * 2 * _nbytes((tm, d), F32) + 2 * 2 * _nbytes((tm, ATT_WIDTH), BF16)
           + 2 * _nbytes((d, d), BF16) + 2 * _nbytes((tm, d), F32))
    row = lambda i: (i, 0)
    return pl.pallas_call(
        _out_proj_kernel,
        grid=(n // tm,),
        in_specs=[
            pl.BlockSpec((tm, d), row),
            pl.BlockSpec((tm, ATT_WIDTH), row),
            pl.BlockSpec((tm, MLSTM_WIDTH), row),
            pl.BlockSpec((ATT_WIDTH, d), lambda i: (0, 0)),
            pl.BlockSpec((MLSTM_WIDTH, d), lambda i: (1, 0)),
        ],
        out_specs=pl.BlockSpec((tm, d), row),
        out_shape=jax.ShapeDtypeStruct((n, d), F32),
        compiler_params=_params(("parallel",), est),
        name="out_proj",
    )(x, y_att, y_mem, w_out, w_out)


def _ple_final_kernel(x_ref, p_ref, nw_ref, wg_ref, wp_ref, fw_ref, o_ref):
    x = x_ref[...]
    h = _rms(x, nw_ref[...]).astype(BF16)
    gate = jax.nn.sigmoid(jnp.dot(h, wg_ref[...], preferred_element_type=F32))
    proj = jnp.dot(p_ref[...].astype(BF16), wp_ref[...], preferred_element_type=F32)
    o_ref[...] = _rms(x + gate * proj, fw_ref[...])


def _ple_final(x, p, norm_w, w_gate, w_proj, final_w, *, tm=512):
    n, d = x.shape
    est = (2 * 2 * _nbytes((tm, d), F32) + 2 * _nbytes((tm, PLE_DIM), F32) + 2 * _nbytes((d, d), BF16)
           + 2 * _nbytes((PLE_DIM, d), BF16) + 4 * _nbytes((tm, d), F32))
    row = lambda i: (i, 0)
    fixed = lambda i: (0, 0)
    return pl.pallas_call(
        _ple_final_kernel,
        grid=(n // tm,),
        in_specs=[
            pl.BlockSpec((tm, d), row),
            pl.BlockSpec((tm, PLE_DIM), row),
            pl.BlockSpec((1, d), fixed),
            pl.BlockSpec((d, d), fixed),
            pl.BlockSpec((PLE_DIM, d), fixed),
            pl.BlockSpec((1, d), fixed),
        ],
        out_specs=pl.BlockSpec((tm, d), row),
        out_shape=jax.ShapeDtypeStruct((n, d), F32),
        compiler_params=_params(("parallel",), est),
        name="ple_final",
    )(x, p, norm_w.reshape(1, d), w_gate, w_proj, final_w.reshape(1, d))


def _trunk(x, p, wts):
    b, s, d = x.shape
    n = b * s
    h = _ffn(x.reshape(n, d), wts["ffn1_norm"], wts["ffn1_w_in"], wts["ffn1_w_out"])
    *atts, mqk, mv, mo, gates_t = _in_proj(h, wts["mix_norm"], wts["w_in_main"], wts["w_in_gates_t"],
                                           wts["b_gates"], seq=s)
    nums, stats = zip(*[_attn_branch(att, wts["bias"][d_]) for att, d_ in zip(atts, DILATIONS)])
    y_att = _attn_merge(nums, stats, wts["attn_out_norm"])
    qk = _conv_silu(mqk.reshape(b, s, 2 * MLSTM_WIDTH), wts["conv_w"], wts["conv_b"])
    y_mem = _mlstm(qk, mv.reshape(b, s, MLSTM_WIDTH), gates_t, mo.reshape(b, s, MLSTM_WIDTH),
                   wts["mlstm_out_norm"])
    h = _out_proj(h, y_att, y_mem.reshape(n, MLSTM_WIDTH), wts["w_out"])
    h = _ffn(h, wts["ffn2_norm"], wts["ffn2_w_in"], wts["ffn2_w_out"])
    y = _ple_final(h, p.reshape(n, PLE_DIM), wts["ple_norm"], wts["ple_w_gate"], wts["ple_w_proj"],
                   wts["final_norm"])
    return y.reshape(b, s, d)


def kernel(x_prompt, x_sample, p_prompt, p_sample, rel_table, ffn1_norm, ffn1_w_in, ffn1_w_out, mix_norm, w_in, b_gates, conv_w, conv_b, attn_out_norm, mlstm_out_norm, w_out, ffn2_norm, ffn2_w_in, ffn2_w_out, ple_norm, ple_w_gate, ple_w_proj, final_norm):
    depth = ffn1_norm.shape[0]
    assert depth == 1
    i = 0
    n_main = 3 * ATT_WIDTH + 4 * MLSTM_WIDTH
    col_scale = jnp.concatenate([jnp.full((ATT_WIDTH,), ATT_HEAD_DIM ** -0.5 * LOG2_E, F32),
                                 jnp.ones((n_main - ATT_WIDTH,), F32)])
    wts = dict(
        ffn1_norm=ffn1_norm[i], ffn1_w_in=ffn1_w_in[i].astype(BF16), ffn1_w_out=ffn1_w_out[i].astype(BF16),
        mix_norm=mix_norm[i],
        w_in_main=(w_in[i][:, :n_main] * col_scale).astype(BF16),
        w_in_gates_t=w_in[i][:, n_main:].T.astype(BF16),
        b_gates=b_gates[i], conv_w=conv_w[i], conv_b=conv_b[i],
        attn_out_norm=attn_out_norm[i], mlstm_out_norm=mlstm_out_norm[i],
        w_out=w_out[i].astype(BF16),
        ffn2_norm=ffn2_norm[i], ffn2_w_in=ffn2_w_in[i].astype(BF16), ffn2_w_out=ffn2_w_out[i].astype(BF16),
        ple_norm=ple_norm[i], ple_w_gate=ple_w_gate[i].astype(BF16), ple_w_proj=ple_w_proj[i].astype(BF16),
        final_norm=final_norm,
        bias={d_: _branch_bias(rel_table, d_) for d_ in DILATIONS},
    )
    return (_trunk(x_prompt, p_prompt[i], wts), _trunk(x_sample, p_sample[i], wts))
```

```python
import functools
import math

import jax
import jax.numpy as jnp
import numpy as np
from jax import lax
from jax.experimental import pallas as pl
from jax.experimental.pallas import tpu as pltpu

F32 = jnp.float32
BF16 = jnp.bfloat16

D_MODEL = 2048
ATT_HEADS = 16
ATT_HEAD_DIM = 64
ATT_WIDTH = ATT_HEADS * ATT_HEAD_DIM
DILATIONS = (1, 4, 16)
HALF_WINDOW = 64
N_BUCKETS = 32
MAX_DISTANCE = 1024
MLSTM_HEADS = 4
MLSTM_HEAD_DIM = 256
MLSTM_WIDTH = MLSTM_HEADS * MLSTM_HEAD_DIM
N_GATES = 4 * MLSTM_HEADS
D_FF = 5632
PLE_DIM = 256
EPS = 1e-6
NEG = -1e30
LOG2_E = math.log2(math.e)

LANES = 128
V7X_VMEM_BYTES = 64 * 1024 * 1024
VMEM_CAP = V7X_VMEM_BYTES - 6 * 1024 * 1024
VMEM_SLACK = 8 * 1024 * 1024

Q_BLOCK = 128
K_BLOCK = Q_BLOCK + 2 * HALF_WINDOW
ATT_UNROLL = 16
MLSTM_CHUNK = {False: 256, True: 128}
NORM_ROWS = 64
MLSTM_UNROLL = 1

NT_DIMS = (((1,), (1,)), ((), ()))
TN_DIMS = (((0,), (0,)), ((), ()))


def _params(semantics, est_bytes):
    limit = int(min(est_bytes + VMEM_SLACK, VMEM_CAP))
    return pltpu.CompilerParams(dimension_semantics=semantics, vmem_limit_bytes=limit)


def _nbytes(shape, dtype):
    return math.prod(shape) * jnp.dtype(dtype).itemsize


def _rms(x, w):
    ms = jnp.mean(x * x, axis=-1, keepdims=True)
    return x * lax.rsqrt(ms + EPS) * w


def _ffn_kernel(x_ref, nw_ref, wg_ref, wu_ref, wo_ref, o_ref, h_ref):
    f = pl.program_id(1)

    @pl.when(f == 0)
    def _():
        x = x_ref[...]
        h_ref[...] = _rms(x, nw_ref[...]).astype(BF16)
        o_ref[...] = x

    h = h_ref[...]
    g = jnp.dot(h, wg_ref[...], preferred_element_type=F32)
    u = jnp.dot(h, wu_ref[...], preferred_element_type=F32)
    a = (g * jax.nn.sigmoid(g) * u * 0.5).astype(BF16)
    o_ref[...] += jnp.dot(a, wo_ref[...], preferred_element_type=F32)


def _ffn(x, norm_w, w_in, w_out, *, tm=1024, tf=512):
    n, d = x.shape
    nf = D_FF // tf
    est = (2 * 2 * _nbytes((tm, d), F32) + _nbytes((tm, d), BF16)
           + 2 * 3 * _nbytes((d, tf), BF16) + 3 * _nbytes((tm, tf), F32))
    return pl.pallas_call(
        _ffn_kernel,
        grid=(n // tm, nf),
        in_specs=[
            pl.BlockSpec((tm, d), lambda i, f: (i, 0)),
            pl.BlockSpec((1, d), lambda i, f: (0, 0)),
            pl.BlockSpec((d, tf), lambda i, f: (0, f)),
            pl.BlockSpec((d, tf), lambda i, f: (0, f + nf)),
            pl.BlockSpec((tf, d), lambda i, f: (f, 0)),
        ],
        out_specs=pl.BlockSpec((tm, d), lambda i, f: (i, 0)),
        out_shape=jax.ShapeDtypeStruct((n, d), F32),
        scratch_shapes=[pltpu.VMEM((tm, d), BF16)],
        compiler_params=_params(("parallel", "arbitrary"), est),
        name="ffn",
    )(x, norm_w.reshape(1, d), w_in, w_in, w_out)


IN_TN = 512
NB_ATT = 3 * ATT_WIDTH // IN_TN
NB_MQK = 2 * MLSTM_WIDTH // IN_TN
NB_MV = MLSTM_WIDTH // IN_TN
NB_MO = MLSTM_WIDTH // IN_TN
IN_PARTS = 4


def _in_proj_kernel(x_ref, nw_ref, w_ref, wgt_ref, bg_ref,
                    att_ref, att4_ref, att16_ref, mqk_ref, mv_ref, mo_ref, gt_ref, h_ref, zs_ref):
    j = pl.program_id(1)
    tm = x_ref.shape[0]

    @pl.when(j == 0)
    def _():
        h = _rms(x_ref[...], nw_ref[...]).astype(BF16)
        h_ref[...] = h
        gt_ref[...] = lax.dot_general(wgt_ref[...], h, NT_DIMS, preferred_element_type=F32) + bg_ref[...]

    @pl.when(j < NB_ATT)
    def _():
        part = tm // IN_PARTS
        for c in range(IN_PARTS):
            z = jnp.dot(h_ref[c * part:(c + 1) * part, :], w_ref[...], preferred_element_type=F32)
            att_ref[0, 0, c * part:(c + 1) * part, :] = z.astype(BF16)
            for s in range(IN_TN // LANES):
                zs_ref[s, c * part:(c + 1) * part, :] = z[:, s * LANES:(s + 1) * LANES]
            for d, ref in ((DILATIONS[1], att4_ref), (DILATIONS[2], att16_ref)):
                rows = part // d
                for r in range(d):
                    for s in range(IN_TN // LANES):
                        piece = zs_ref[s, pl.ds(c * part + r, rows, stride=d), :]
                        ref[0, r, c * rows:(c + 1) * rows, s * LANES:(s + 1) * LANES] = piece.astype(BF16)

    def plain(ref):
        part = tm // IN_PARTS
        for c in range(IN_PARTS):
            rows = slice(c * part, (c + 1) * part)
            ref[rows, :] = jnp.dot(h_ref[rows, :], w_ref[...], preferred_element_type=F32).astype(ref.dtype)

    @pl.when((j >= NB_ATT) & (j < NB_ATT + NB_MQK))
    def _():
        plain(mqk_ref)

    @pl.when((j >= NB_ATT + NB_MQK) & (j < NB_ATT + NB_MQK + NB_MV))
    def _():
        plain(mv_ref)

    @pl.when(j >= NB_ATT + NB_MQK + NB_MV)
    def _():
        plain(mo_ref)


def _in_proj(x, norm_w, w_main, w_gates_t, b_gates, *, seq, tm=1024):
    n, d = x.shape
    tn = IN_TN
    nb = NB_ATT + NB_MQK + NB_MV + NB_MO
    o1, o2, o3 = NB_ATT, NB_ATT + NB_MQK, NB_ATT + NB_MQK + NB_MV
    batch, tpb = n // seq, seq // tm

    def col(off, cnt):
        return lambda i, j: (i, jnp.clip(j - off, 0, cnt - 1))

    def att_spec(dil):
        return pl.BlockSpec((1, dil, tm // dil, tn),
                            lambda i, j: (i // tpb, 0, i % tpb, jnp.minimum(j, NB_ATT - 1)))

    def att_shape(dil):
        return jax.ShapeDtypeStruct((batch, dil, seq // dil, 3 * ATT_WIDTH), BF16)

    est = (2 * _nbytes((tm, d), F32) + _nbytes((tm, d), BF16) + 2 * _nbytes((d, tn), BF16)
           + 2 * (2 * _nbytes((tm, tn), F32) + 4 * _nbytes((tm, tn), BF16)) + 3 * _nbytes((tm, tn), F32))
    return pl.pallas_call(
        _in_proj_kernel,
        grid=(n // tm, nb),
        in_specs=[
            pl.BlockSpec((tm, d), lambda i, j: (i, 0)),
            pl.BlockSpec((1, d), lambda i, j: (0, 0)),
            pl.BlockSpec((d, tn), lambda i, j: (0, j)),
            pl.BlockSpec((N_GATES, d), lambda i, j: (0, 0)),
            pl.BlockSpec((N_GATES, 1), lambda i, j: (0, 0)),
        ],
        out_specs=[
            att_spec(DILATIONS[0]),
            att_spec(DILATIONS[1]),
            att_spec(DILATIONS[2]),
            pl.BlockSpec((tm, tn), col(o1, NB_MQK)),
            pl.BlockSpec((tm, tn), col(o2, NB_MV)),
            pl.BlockSpec((tm, tn), col(o3, NB_MO)),
            pl.BlockSpec((N_GATES, tm), lambda i, j: (0, i)),
        ],
        out_shape=[
            att_shape(DILATIONS[0]),
            att_shape(DILATIONS[1]),
            att_shape(DILATIONS[2]),
            jax.ShapeDtypeStruct((n, 2 * MLSTM_WIDTH), F32),
            jax.ShapeDtypeStruct((n, MLSTM_WIDTH), BF16),
            jax.ShapeDtypeStruct((n, MLSTM_WIDTH), F32),
            jax.ShapeDtypeStruct((N_GATES, n), F32),
        ],
        scratch_shapes=[pltpu.VMEM((tm, d), BF16), pltpu.VMEM((tn // LANES, tm, LANES), F32)],
        compiler_params=_params(("parallel", "arbitrary"), est),
        name="in_proj",
    )(x, norm_w.reshape(1, d), w_main, w_gates_t, b_gates.reshape(N_GATES, 1))


def _rel_bucket(rel):
    nb = N_BUCKETS // 2
    max_exact = nb // 2
    n = np.abs(rel)
    large = max_exact + (np.log(np.maximum(n, 1) / max_exact) / math.log(MAX_DISTANCE / max_exact)
                         * (nb - max_exact)).astype(np.int32)
    large = np.minimum(large, nb - 1)
    return np.where(rel > 0, nb, 0) + np.where(n < max_exact, n, large)


def _branch_bias(rel_table, dilation):
    key = np.arange(K_BLOCK)[None, :]
    off = key - HALF_WINDOW - np.arange(Q_BLOCK)[:, None]
    band = np.abs(off) <= HALF_WINDOW
    keep = np.stack([band & ((v & 1 == 0) | (key >= HALF_WINDOW)) & ((v & 2 == 0) | (key < K_BLOCK - HALF_WINDOW))
                     for v in range(4)])
    onehot = (_rel_bucket(off * dilation).reshape(1, -1) == np.arange(N_BUCKETS)[:, None]).astype(np.float32)
    bias = jnp.einsum("bh,bn->hn", rel_table.astype(F32) * LOG2_E, jnp.asarray(onehot),
                      precision=lax.Precision.HIGHEST)
    return jnp.where(keep[:, None], bias.reshape(1, ATT_HEADS, Q_BLOCK, K_BLOCK), NEG)


def _attn_branch_kernel(q_ref, k_ref, v_ref, bias_ref, o_ref, st_ref, kpad, vpad, *, seq, width):
    pw = pl.program_id(2)
    pairs = width // LANES

    zeros = jnp.zeros((HALF_WINDOW, width), BF16)
    kpad[0:HALF_WINDOW, :] = zeros
    vpad[0:HALF_WINDOW, :] = zeros
    kpad[HALF_WINDOW + seq:2 * HALF_WINDOW + seq, :] = zeros
    vpad[HALF_WINDOW + seq:2 * HALF_WINDOW + seq, :] = zeros
    kpad[HALF_WINDOW:HALF_WINDOW + seq, :] = k_ref[0, 0]
    vpad[HALF_WINDOW:HALF_WINDOW + seq, :] = v_ref[0, 0]

    @pl.when(pw == 0)
    def _():
        st_ref[...] = jnp.zeros_like(st_ref)

    lane = lax.broadcasted_iota(jnp.int32, (1, LANES), 1)
    low = lane < ATT_HEAD_DIM
    n_blocks = seq // Q_BLOCK
    unroll = math.gcd(n_blocks, ATT_UNROLL)

    for p in range(pairs):
        sl = slice(p * LANES, (p + 1) * LANES)
        head_pair = pw * pairs + p

        def block(q0, sl=sl, p=p, head_pair=head_pair):
            qb = q_ref[0, 0, pl.ds(q0, Q_BLOCK), sl]
            kw = kpad[pl.ds(q0, K_BLOCK), sl]
            vw = vpad[pl.ds(q0, K_BLOCK), sl]
            if isinstance(q0, int):
                var = int(q0 == 0) + 2 * int(q0 == seq - Q_BLOCK)
            else:
                var = jnp.where(q0 == 0, 1, 0) + jnp.where(q0 == seq - Q_BLOCK, 2, 0)
            zero = jnp.zeros_like(qb)
            qs = jnp.concatenate([jnp.where(low, qb, zero), jnp.where(low, zero, qb)], axis=0)
            s = lax.dot_general(qs, kw, NT_DIMS, preferred_element_type=F32)
            s0 = s[:Q_BLOCK] + bias_ref[var, 2 * p]
            s1 = s[Q_BLOCK:] + bias_ref[var, 2 * p + 1]
            m0 = jnp.max(s0, axis=1, keepdims=True)
            m1 = jnp.max(s1, axis=1, keepdims=True)
            e0 = jnp.exp2(s0 - m0)
            e1 = jnp.exp2(s1 - m1)
            l0 = jnp.sum(e0, axis=1, keepdims=True)
            l1 = jnp.sum(e1, axis=1, keepdims=True)
            e = jnp.concatenate([e0, e1], axis=0).astype(BF16)
            pv = jnp.dot(e, vw, preferred_element_type=F32)
            o_ref[0, 0, pl.ds(q0, Q_BLOCK), sl] = jnp.where(low, pv[:Q_BLOCK], pv[Q_BLOCK:])
            h0 = 2 * head_pair
            prev = st_ref[0, 0, pl.ds(q0, Q_BLOCK), :]
            st_ref[0, 0, pl.ds(q0, Q_BLOCK), :] = jnp.where(
                lane == h0, m0, jnp.where(
                    lane == h0 + 1, m1, jnp.where(
                        lane == ATT_HEADS + h0, l0, jnp.where(lane == ATT_HEADS + h0 + 1, l1, prev))))

        if n_blocks < ATT_UNROLL:
            for i in range(n_blocks):
                block(i * Q_BLOCK)
            continue

        def body(i, carry, block=block):
            for u in range(unroll):
                block(pl.multiple_of((i * unroll + u) * Q_BLOCK, Q_BLOCK))
            return carry

        lax.fori_loop(0, n_blocks // unroll, body, 0)


def _attn_width(seq):
    return int(min(ATT_WIDTH, max(LANES, (1 << 20) // seq // LANES * LANES)))


def _attn_branch(att, bias):
    b, d, seq, _ = att.shape
    width = _attn_width(seq)
    npw = ATT_WIDTH // width
    est = (2 * 3 * _nbytes((seq, width), BF16) + 2 * _nbytes((4, width // ATT_HEAD_DIM, Q_BLOCK, K_BLOCK), F32)
           + 2 * _nbytes((seq, width), F32) + 2 * _nbytes((seq, LANES), F32)
           + 2 * _nbytes((seq + 2 * HALF_WINDOW, width), BF16))
    kern = functools.partial(_attn_branch_kernel, seq=seq, width=width)
    return pl.pallas_call(
        kern,
        grid=(b, d, npw),
        in_specs=[
            pl.BlockSpec((1, 1, seq, width), lambda bi, r, pw: (bi, r, 0, pw)),
            pl.BlockSpec((1, 1, seq, width), lambda bi, r, pw: (bi, r, 0, npw + pw)),
            pl.BlockSpec((1, 1, seq, width), lambda bi, r, pw: (bi, r, 0, 2 * npw + pw)),
            pl.BlockSpec((4, width // ATT_HEAD_DIM, Q_BLOCK, K_BLOCK), lambda bi, r, pw: (0, pw, 0, 0)),
        ],
        out_specs=[
            pl.BlockSpec((1, 1, seq, width), lambda bi, r, pw: (bi, r, 0, pw)),
            pl.BlockSpec((1, 1, seq, LANES), lambda bi, r, pw: (bi, r, 0, 0)),
        ],
        out_shape=[
            jax.ShapeDtypeStruct((b, d, seq, ATT_WIDTH), F32),
            jax.ShapeDtypeStruct((b, d, seq, LANES), F32),
        ],
        scratch_shapes=[pltpu.VMEM((seq + 2 * HALF_WINDOW, width), BF16),
                        pltpu.VMEM((seq + 2 * HALF_WINDOW, width), BF16)],
        compiler_params=_params(("parallel", "parallel", "arbitrary"), est),
        name=f"attn_d{d}",
    )(att, att, att, bias)


def _attn_merge_kernel(n1_ref, n4_ref, n16_ref, s1_ref, s4_ref, s16_ref, ex_ref, nw_ref, y_ref,
                       t4_ref, t16_ref, out_ref):
    tm = y_ref.shape[0]
    lane = lax.broadcasted_iota(jnp.int32, (1, LANES), 1)
    head = lane < ATT_HEADS

    def natural(ref, tmp_ref, s):
        d = ref.shape[1]
        if d == 1:
            return ref[0, 0, :, s * LANES:(s + 1) * LANES]
        for r in range(d):
            tmp_ref[pl.ds(r, tm // d, stride=d), :] = ref[0, r, :, s * LANES:(s + 1) * LANES]
        return tmp_ref[...]

    st = [natural(s1_ref, None, 0), natural(s4_ref, t4_ref, 0), natural(s16_ref, t16_ref, 0)]
    m = jnp.maximum(jnp.maximum(st[0], st[1]), st[2])
    scale = [jnp.exp2(x - m) for x in st]
    den = sum(sc * pltpu.roll(x, LANES - ATT_HEADS, axis=1) for sc, x in zip(scale, st))
    inv = 1.0 / den
    ex = ex_ref[...]

    def spread(w):
        w = jnp.where(head, w, 0.0)
        hi = w.astype(BF16)
        lo = (w - hi.astype(F32)).astype(BF16)
        return (jnp.dot(hi, ex, preferred_element_type=F32) + jnp.dot(lo, ex, preferred_element_type=F32))

    wide = [spread(sc * inv) for sc in scale]
    refs = ((n1_ref, None), (n4_ref, t4_ref), (n16_ref, t16_ref))
    for s in range(ATT_WIDTH // LANES):
        sl = slice(s * LANES, (s + 1) * LANES)
        out_ref[:, sl] = sum(w[:, sl] * natural(ref, tmp, s) for w, (ref, tmp) in zip(wide, refs))
    y_ref[...] = _rms(out_ref[...], nw_ref[...]).astype(BF16)


def _attn_merge(nums, stats, norm_w, *, tm=512):
    b, _, s, _ = nums[0].shape
    n, tpb = b * s, s // tm
    expand = np.zeros((LANES, ATT_WIDTH), np.float32)
    for h in range(ATT_HEADS):
        expand[h, h * ATT_HEAD_DIM:(h + 1) * ATT_HEAD_DIM] = 1.0
    est = (2 * 3 * (_nbytes((tm, ATT_WIDTH), F32) + _nbytes((tm, LANES), F32)) + 5 * _nbytes((tm, ATT_WIDTH), F32)
           + 2 * _nbytes((LANES, ATT_WIDTH), F32))

    def blk(d, c):
        return pl.BlockSpec((1, d, tm // d, c), lambda i: (i // tpb, 0, i % tpb, 0))

    fixed = lambda i: (0, 0)
    return pl.pallas_call(
        _attn_merge_kernel,
        grid=(n // tm,),
        in_specs=[blk(d, ATT_WIDTH) for d in DILATIONS] + [blk(d, LANES) for d in DILATIONS] + [
            pl.BlockSpec((LANES, ATT_WIDTH), fixed), pl.BlockSpec((1, ATT_WIDTH), fixed)],
        out_specs=pl.BlockSpec((tm, ATT_WIDTH), lambda i: (i, 0)),
        out_shape=jax.ShapeDtypeStruct((n, ATT_WIDTH), BF16),
        scratch_shapes=[pltpu.VMEM((tm, LANES), F32), pltpu.VMEM((tm, LANES), F32),
                        pltpu.VMEM((tm, ATT_WIDTH), F32)],
        compiler_params=_params(("parallel",), est),
        name="attn_merge",
    )(*nums, *stats, jnp.asarray(expand, dtype=BF16), norm_w.reshape(1, ATT_WIDTH))


CONV_ROWS = 512
CONV_COLS = 256


def _conv_silu_kernel(x_ref, w_ref, b_ref, y_ref, *, seq):
    cb = pl.program_id(1)
    scale = jnp.where(cb >= MLSTM_WIDTH // CONV_COLS, MLSTM_HEAD_DIM ** -0.5, 1.0).astype(F32)
    w0, w1, w2 = w_ref[0:1, :], w_ref[1:2, :], w_ref[2:3, :]
    bias = b_ref[...]
    rows = lax.broadcasted_iota(jnp.int32, (CONV_ROWS, 1), 0)

    def body(c, carry):
        c0 = pl.multiple_of(c * CONV_ROWS, CONV_ROWS)
        xc = x_ref[0, pl.ds(c0, CONV_ROWS), :]
        before = x_ref[0, pl.ds(pl.multiple_of(jnp.maximum(c0 - 8, 0), 8), 8), :][7:8, :]
        after = x_ref[0, pl.ds(pl.multiple_of(jnp.minimum(c0 + CONV_ROWS, seq - 8), 8), 8), :][0:1, :]
        before = jnp.where(c0 > 0, before, 0.0)
        after = jnp.where(c0 + CONV_ROWS < seq, after, 0.0)
        up = jnp.where(rows == 0, before, pltpu.roll(xc, 1, axis=0))
        dn = jnp.where(rows == CONV_ROWS - 1, after, pltpu.roll(xc, CONV_ROWS - 1, axis=0))
        y = w0 * up + w1 * xc + w2 * dn + bias
        y_ref[0, pl.ds(c0, CONV_ROWS), :] = (y * jax.nn.sigmoid(y) * scale).astype(BF16)
        return carry

    lax.fori_loop(0, seq // CONV_ROWS, body, 0)


def _conv_silu(x, conv_w, conv_b):
    b, s, c = x.shape
    est = 2 * _nbytes((s, CONV_COLS), F32) + 2 * _nbytes((s, CONV_COLS), BF16) + 8 * _nbytes((CONV_ROWS, CONV_COLS), F32)
    return pl.pallas_call(
        functools.partial(_conv_silu_kernel, seq=s),
        grid=(b, c // CONV_COLS),
        in_specs=[
            pl.BlockSpec((1, s, CONV_COLS), lambda bi, cb: (bi, 0, cb)),
            pl.BlockSpec((3, CONV_COLS), lambda bi, cb: (0, cb)),
            pl.BlockSpec((1, CONV_COLS), lambda bi, cb: (0, cb)),
        ],
        out_specs=pl.BlockSpec((1, s, CONV_COLS), lambda bi, cb: (bi, 0, cb)),
        out_shape=jax.ShapeDtypeStruct((b, s, c), BF16),
        compiler_params=_params(("parallel", "parallel"), est),
        name="conv_silu",
    )(x, conv_w, conv_b.reshape(1, c))


EXT = MLSTM_HEAD_DIM + LANES


def _log_sigmoid(x):
    return jnp.minimum(x, 0.0) - jnp.log1p(jnp.exp(-jnp.abs(x)))


def _mlstm_kernel(*refs, reverse, tile):
    if reverse:
        q_ref, k_ref, v_ref, gt_ref, hf_ref, o_ref, nw_ref, out_ref, cext_ref, m_ref = refs
    else:
        q_ref, k_ref, v_ref, gt_ref, out_ref, cext_ref, m_ref = refs
    C = MLSTM_CHUNK[reverse]
    nchunks = tile // C

    @pl.when(pl.program_id(1) == 0)
    def _():
        cext_ref[...] = jnp.zeros_like(cext_ref)
        m_ref[...] = jnp.full_like(m_ref, NEG)

    r_i = lax.broadcasted_iota(jnp.int32, (C, C), 0)
    c_i = lax.broadcasted_iota(jnp.int32, (C, C), 1)
    if reverse:
        causal, causal_t = c_i >= r_i, r_i >= c_i
    else:
        causal, causal_t = c_i <= r_i, r_i <= c_i
    tri_row = jnp.where(causal_t, 1.0, 0.0).astype(F32)
    tri_col = jnp.where(causal, 1.0, 0.0).astype(F32)
    eye = jnp.where(r_i == c_i, 1.0, 0.0).astype(F32)
    ones_lane = jnp.ones((C, LANES), BF16)
    hi = lax.Precision.HIGHEST
    g_i = 2 * MLSTM_HEADS if reverse else 0
    g_f = g_i + MLSTM_HEADS

    def chunk(ci):
        c = (nchunks - 1 - ci) if reverse else ci
        t0 = pl.multiple_of(c * C, C)
        gates = gt_ref[:, pl.ds(t0, C)]
        lsig = _log_sigmoid(gates)
        cum_rows = jnp.dot(lsig, tri_row, preferred_element_type=F32, precision=hi)
        cum_cols = lax.dot_general(tri_col, lsig, NT_DIMS, preferred_element_type=F32, precision=hi)
        gate_cols = lax.dot_general(eye, gates, NT_DIMS, preferred_element_type=F32, precision=hi)
        li, li_cols = gates[g_i:g_i + MLSTM_HEADS], gate_cols[:, g_i:g_i + MLSTM_HEADS]
        b_rows, b_cols = cum_rows[g_f:g_f + MLSTM_HEADS], cum_cols[:, g_f:g_f + MLSTM_HEADS]
        for hd in range(MLSTM_HEADS):
            hs = slice(hd * MLSTM_HEAD_DIM, (hd + 1) * MLSTM_HEAD_DIM)
            q = q_ref[0, pl.ds(t0, C), hs]
            k = k_ref[0, pl.ds(t0, C), hs]
            v = v_ref[0, pl.ds(t0, C), hs]
            vext = jnp.concatenate([v, ones_lane], axis=1)
            bc, br = b_cols[:, hd:hd + 1], b_rows[hd:hd + 1, :]
            lir, lic = li[hd:hd + 1, :], li_cols[:, hd:hd + 1]
            m_prev = m_ref[hd, 0:1, 0:1]
            dm = jnp.where(causal, bc - br + lir, -jnp.inf)
            inter = bc + m_prev
            m_t = jnp.maximum(inter, jnp.max(dm, axis=1, keepdims=True))
            sqk = lax.dot_general(q, k, NT_DIMS, preferred_element_type=F32)
            w_intra = (jnp.exp(dm - m_t) * sqk).astype(BF16)
            w_inter = jnp.exp(inter - m_t)
            cext = cext_ref[hd]
            num = (jnp.dot(w_intra, vext, preferred_element_type=F32)
                   + w_inter * jnp.dot(q, cext.astype(BF16), preferred_element_type=F32))
            den =jnp.concatenate([num[:, MLSTM_HEAD_DIM:]] * (MLSTM_HEAD_DIM // LANES), axis=1)
            h = num[:, :MLSTM_HEAD_DIM] / jnp.maximum(jnp.abs(den), jnp.exp(-m_t))
            g = br[:, 0:1] if reverse else br[:, C - 1:C]
            m_new = jnp.maximum(g + m_prev, jnp.max(g - br + lir, axis=1, keepdims=True))
            w_a = jnp.exp(g - bc + lic - m_new)
            decay = jnp.exp(g + m_prev - m_new)
            vw = (vext.astype(F32) * w_a).astype(BF16)
            cext_ref[hd] = decay * cext + lax.dot_general(k, vw, TN_DIMS, preferred_element_type=F32)
            m_ref[hd] = jnp.broadcast_to(m_new, m_ref.shape[1:])
            if reverse:
                for r0 in range(0, C, NORM_ROWS):
                    rows = pl.ds(t0 + r0, NORM_ROWS)
                    hsum = h[r0:r0 + NORM_ROWS] + hf_ref[0, rows, hs]
                    mu = jnp.mean(hsum, axis=1, keepdims=True)
                    dev = hsum - mu
                    var = jnp.mean(dev * dev, axis=1, keepdims=True)
                    y = dev * lax.rsqrt(var + EPS) * nw_ref[:, hs] * jax.nn.sigmoid(o_ref[0, rows, hs])
                    out_ref[0, rows, hs] = y.astype(BF16)
            else:
                out_ref[0, pl.ds(t0, C), hs] = h

    def body(i, carry):
        for u in range(MLSTM_UNROLL):
            chunk(i * MLSTM_UNROLL + u)
        return carry

    lax.fori_loop(0, nchunks // MLSTM_UNROLL, body, 0)


def _mlstm(qk, v, gates_t, o_pre, norm_w, *, tile=512):
    b, s, _ = v.shape
    ns = s // tile
    w = MLSTM_WIDTH
    scratch = [pltpu.VMEM((MLSTM_HEADS, MLSTM_HEAD_DIM, EXT), F32), pltpu.VMEM((MLSTM_HEADS, 8, LANES), F32)]
    est = (2 * 3 * _nbytes((tile, w), BF16) + 2 * 3 * _nbytes((tile, w), F32)
           + _nbytes((MLSTM_HEADS, MLSTM_HEAD_DIM, EXT), F32) + 16 * _nbytes((max(MLSTM_CHUNK.values()), EXT), F32))

    def specs(order):
        return [
            pl.BlockSpec((1, tile, w), lambda bi, si: (bi, order(si), 0)),
            pl.BlockSpec((1, tile, w), lambda bi, si: (bi, order(si), 1)),
            pl.BlockSpec((1, tile, w), lambda bi, si: (bi, order(si), 0)),
            pl.BlockSpec((N_GATES, tile), lambda bi, si: (0, bi * ns + order(si))),
        ]

    fwd = lambda si: si
    bwd = lambda si: ns - 1 - si
    h_fwd = pl.pallas_call(
        functools.partial(_mlstm_kernel, reverse=False, tile=tile),
        grid=(b, ns),
        in_specs=specs(fwd),
        out_specs=pl.BlockSpec((1, tile, w), lambda bi, si: (bi, si, 0)),
        out_shape=jax.ShapeDtypeStruct((b, s, w), F32),
        scratch_shapes=scratch,
        compiler_params=_params(("parallel", "arbitrary"), est),
        name="mlstm_fwd",
    )(qk, qk, v, gates_t)
    return pl.pallas_call(
        functools.partial(_mlstm_kernel, reverse=True, tile=tile),
        grid=(b, ns),
        in_specs=specs(bwd) + [
            pl.BlockSpec((1, tile, w), lambda bi, si: (bi, bwd(si), 0)),
            pl.BlockSpec((1, tile, w), lambda bi, si: (bi, bwd(si), 0)),
            pl.BlockSpec((1, w), lambda bi, si: (0, 0)),
        ],
        out_specs=pl.BlockSpec((1, tile, w), lambda bi, si: (bi, bwd(si), 0)),
        out_shape=jax.ShapeDtypeStruct((b, s, w), BF16),
        scratch_shapes=scratch,
        compiler_params=_params(("parallel", "arbitrary"), est),
        name="mlstm_bwd",
    )(qk, qk, v, gates_t, h_fwd, o_pre, norm_w.reshape(1, w))


def _out_proj_kernel(x_ref, ya_ref, ym_ref, wa_ref, wm_ref, o_ref):
    o_ref[...] = (x_ref[...]
                  + jnp.dot(ya_ref[...], wa_ref[...], preferred_element_type=F32)
                  + jnp.dot(ym_ref[...], wm_ref[...], preferred_element_type=F32))


def _out_proj(x, y_att, y_mem, w_out, *, tm=512):
    n, d = x.shape
    est = (2 * 2 * _nbytes((tm, d), F32) + 2 * 2 * _nbytes((tm, ATT_WIDTH), BF16)
           + 2 * _nbytes((d, d), BF16) + 2 * _nbytes((tm, d), F32))
    row = lambda i: (i, 0)
    return pl.pallas_call(
        _out_proj_kernel,
        grid=(n // tm,),
        in_specs=[
            pl.BlockSpec((tm, d), row),
            pl.BlockSpec((tm, ATT_WIDTH), row),
            pl.BlockSpec((tm, MLSTM_WIDTH), row),
            pl.BlockSpec((ATT_WIDTH, d), lambda i: (0, 0)),
            pl.BlockSpec((MLSTM_WIDTH, d), lambda i: (1, 0)),
        ],
        out_specs=pl.BlockSpec((tm, d), row),
        out_shape=jax.ShapeDtypeStruct((n, d), F32),
        compiler_params=_params(("parallel",), est),
        name="out_proj",
    )(x, y_att, y_mem, w_out, w_out)


def _ple_final_kernel(x_ref, p_ref, nw_ref, wg_ref, wp_ref, fw_ref, o_ref):
    x = x_ref[...]
    h = _rms(x, nw_ref[...]).astype(BF16)
    gate = jax.nn.sigmoid(jnp.dot(h, wg_ref[...], preferred_element_type=F32))
    proj = jnp.dot(p_ref[...].astype(BF16), wp_ref[...], preferred_element_type=F32)
    o_ref[...] = _rms(x + gate * proj, fw_ref[...])


def _ple_final(x, p, norm_w, w_gate, w_proj, final_w, *, tm=512):
    n, d = x.shape
    est = (2 * 2 * _nbytes((tm, d), F32) + 2 * _nbytes((tm, PLE_DIM), F32) + 2 * _nbytes((d, d), BF16)
           + 2 * _nbytes((PLE_DIM, d), BF16) + 4 * _nbytes((tm, d), F32))
    row = lambda i: (i, 0)
    fixed = lambda i: (0, 0)
    return pl.pallas_call(
        _ple_final_kernel,
        grid=(n // tm,),
        in_specs=[
            pl.BlockSpec((tm, d), row),
            pl.BlockSpec((tm, PLE_DIM), row),
            pl.BlockSpec((1, d), fixed),
            pl.BlockSpec((d, d), fixed),
            pl.BlockSpec((PLE_DIM, d), fixed),
            pl.BlockSpec((1, d), fixed),
        ],
        out_specs=pl.BlockSpec((tm, d), row),
        out_shape=jax.ShapeDtypeStruct((n, d), F32),
        compiler_params=_params(("parallel",), est),
        name="ple_final",
    )(x, p, norm_w.reshape(1, d), w_gate, w_proj, final_w.reshape(1, d))


def _trunk(x, p, wts):
    b, s, d = x.shape
    n = b * s
    h = _ffn(x.reshape(n, d), wts["ffn1_norm"], wts["ffn1_w_in"], wts["ffn1_w_out"])
    *atts, mqk, mv, mo, gates_t = _in_proj(h, wts["mix_norm"], wts["w_in_main"], wts["w_in_gates_t"],
                                           wts["b_gates"], seq=s)
    nums, stats = zip(*[_attn_branch(att, wts["bias"][d_]) for att, d_ in zip(atts, DILATIONS)])
    y_att = _attn_merge(nums, stats, wts["attn_out_norm"])
    qk = _conv_silu(mqk.reshape(b, s, 2 * MLSTM_WIDTH), wts["conv_w"], wts["conv_b"])
    y_mem = _mlstm(qk, mv.reshape(b, s, MLSTM_WIDTH), gates_t, mo.reshape(b, s, MLSTM_WIDTH),
                   wts["mlstm_out_norm"])
    h = _out_proj(h, y_att, y_mem.reshape(n, MLSTM_WIDTH), wts["w_out"])
    h = _ffn(h, wts["ffn2_norm"], wts["ffn2_w_in"], wts["ffn2_w_out"])
    y = _ple_final(h, p.reshape(n, PLE_DIM), wts["ple_norm"], wts["ple_w_gate"], wts["ple_w_proj"],
                   wts["final_norm"])
    return y.reshape(b, s, d)


def kernel(x_prompt, x_sample, p_prompt, p_sample, rel_table, ffn1_norm, ffn1_w_in, ffn1_w_out, mix_norm, w_in, b_gates, conv_w, conv_b, attn_out_norm, mlstm_out_norm, w_out, ffn2_norm, ffn2_w_in, ffn2_w_out, ple_norm, ple_w_gate, ple_w_proj, final_norm):
    depth = ffn1_norm.shape[0]
    assert depth == 1
    i = 0
    n_main = 3 * ATT_WIDTH + 4 * MLSTM_WIDTH
    col_scale = jnp.concatenate([jnp.full((ATT_WIDTH,), ATT_HEAD_DIM ** -0.5 * LOG2_E, F32),
                                 jnp.ones((n_main - ATT_WIDTH,), F32)])
    wts = dict(
        ffn1_norm=ffn1_norm[i], ffn1_w_in=ffn1_w_in[i].astype(BF16), ffn1_w_out=ffn1_w_out[i].astype(BF16),
        mix_norm=mix_norm[i],
        w_in_main=(w_in[i][:, :n_main] * col_scale).astype(BF16),
        w_in_gates_t=w_in[i][:, n_main:].T.astype(BF16),
        b_gates=b_gates[i], conv_w=conv_w[i], conv_b=conv_b[i],
        attn_out_norm=attn_out_norm[i], mlstm_out_norm=mlstm_out_norm[i],
        w_out=w_out[i].astype(BF16),
        ffn2_norm=ffn2_norm[i], ffn2_w_in=ffn2_w_in[i].astype(BF16), ffn2_w_out=ffn2_w_out[i].astype(BF16),
        ple_norm=ple_norm[i], ple_w_gate=ple_w_gate[i].astype(BF16), ple_w_proj=ple_w_proj[i].astype(BF16),
        final_norm=final_norm,
        bias={d_: _branch_bias(rel_table, d_) for d_ in DILATIONS},
    )
    return (_trunk(x_prompt, p_prompt[i], wts), _trunk(x_sample, p_sample[i], wts))
```

```python
import functools
import math

import jax
import jax.numpy as jnp
import numpy as np
from jax import lax
from jax.experimental import pallas as pl
from jax.experimental.pallas import tpu as pltpu

F32 = jnp.float32
BF16 = jnp.bfloat16

D_MODEL = 2048
ATT_HEADS = 16
ATT_HEAD_DIM = 64
ATT_WIDTH = ATT_HEADS * ATT_HEAD_DIM
DILATIONS = (1, 4, 16)
HALF_WINDOW = 64
N_BUCKETS = 32
MAX_DISTANCE = 1024
MLSTM_HEADS = 4
MLSTM_HEAD_DIM = 256
MLSTM_WIDTH = MLSTM_HEADS * MLSTM_HEAD_DIM
N_GATES = 4 * MLSTM_HEADS
D_FF = 5632
PLE_DIM = 256
EPS = 1e-6
NEG = -1e30
LOG2_E = math.log2(math.e)

LANES = 128
V7X_VMEM_BYTES = 64 * 1024 * 1024
VMEM_CAP = V7X_VMEM_BYTES - 6 * 1024 * 1024
VMEM_SLACK = 8 * 1024 * 1024

Q_BLOCK = 128
K_BLOCK = Q_BLOCK + 2 * HALF_WINDOW
ATT_UNROLL = 16
MLSTM_CHUNK = {False: 256, True: 256}
NORM_ROWS = 128
MLSTM_UNROLL = 1

NT_DIMS = (((1,), (1,)), ((), ()))
TN_DIMS = (((0,), (0,)), ((), ()))


def _params(semantics, est_bytes):
    limit = int(min(est_bytes + VMEM_SLACK, VMEM_CAP))
    return pltpu.CompilerParams(dimension_semantics=semantics, vmem_limit_bytes=limit)


def _nbytes(shape, dtype):
    return math.prod(shape) * jnp.dtype(dtype).itemsize


def _rms(x, w):
    ms = jnp.mean(x * x, axis=-1, keepdims=True)
    return x * lax.rsqrt(ms + EPS) * w


def _ffn_kernel(x_ref, nw_ref, wg_ref, wu_ref, wo_ref, o_ref, h_ref):
    f = pl.program_id(1)

    @pl.when(f == 0)
    def _():
        x = x_ref[...]
        h_ref[...] = _rms(x, nw_ref[...]).astype(BF16)
        o_ref[...] = x

    h = h_ref[...]
    g = jnp.dot(h, wg_ref[...], preferred_element_type=F32)
    u = jnp.dot(h, wu_ref[...], preferred_element_type=F32)
    a = (g * jax.nn.sigmoid(g) * u * 0.5).astype(BF16)
    o_ref[...] += jnp.dot(a, wo_ref[...], preferred_element_type=F32)


def _ffn(x, norm_w, w_in, w_out, *, tm=1024, tf=512):
    n, d = x.shape
    nf = D_FF // tf
    est = (2 * 2 * _nbytes((tm, d), F32) + _nbytes((tm, d), BF16)
           + 2 * 3 * _nbytes((d, tf), BF16) + 3 * _nbytes((tm, tf), F32))
    return pl.pallas_call(
        _ffn_kernel,
        grid=(n // tm, nf),
        in_specs=[
            pl.BlockSpec((tm, d), lambda i, f: (i, 0)),
            pl.BlockSpec((1, d), lambda i, f: (0, 0)),
            pl.BlockSpec((d, tf), lambda i, f: (0, f)),
            pl.BlockSpec((d, tf), lambda i, f: (0, f + nf)),
            pl.BlockSpec((tf, d), lambda i, f: (f, 0)),
        ],
        out_specs=pl.BlockSpec((tm, d), lambda i, f: (i, 0)),
        out_shape=jax.ShapeDtypeStruct((n, d), F32),
        scratch_shapes=[pltpu.VMEM((tm, d), BF16)],
        compiler_params=_params(("parallel", "arbitrary"), est),
        name="ffn",
    )(x, norm_w.reshape(1, d), w_in, w_in, w_out)


IN_TN = 512
NB_ATT = 3 * ATT_WIDTH // IN_TN
NB_MQK = 2 * MLSTM_WIDTH // IN_TN
NB_MV = MLSTM_WIDTH // IN_TN
NB_MO = MLSTM_WIDTH // IN_TN
IN_PARTS = 4


def _in_proj_kernel(x_ref, nw_ref, w_ref, wgt_ref, bg_ref,
                    att_ref, att4_ref, att16_ref, mqk_ref, mv_ref, mo_ref, gt_ref, h_ref, zs_ref):
    j = pl.program_id(1)
    tm = x_ref.shape[0]

    @pl.when(j == 0)
    def _():
        h = _rms(x_ref[...], nw_ref[...]).astype(BF16)
        h_ref[...] = h
        gt_ref[...] = lax.dot_general(wgt_ref[...], h, NT_DIMS, preferred_element_type=F32) + bg_ref[...]

    @pl.when(j < NB_ATT)
    def _():
        part = tm // IN_PARTS
        for c in range(IN_PARTS):
            z = jnp.dot(h_ref[c * part:(c + 1) * part, :], w_ref[...], preferred_element_type=F32)
            att_ref[0, 0, c * part:(c + 1) * part, :] = z.astype(BF16)
            for s in range(IN_TN // LANES):
                zs_ref[s, c * part:(c + 1) * part, :] = z[:, s * LANES:(s + 1) * LANES]
            for d, ref in ((DILATIONS[1], att4_ref), (DILATIONS[2], att16_ref)):
                rows = part // d
                for r in range(d):
                    for s in range(IN_TN // LANES):
                        piece = zs_ref[s, pl.ds(c * part + r, rows, stride=d), :]
                        ref[0, r, c * rows:(c + 1) * rows, s * LANES:(s + 1) * LANES] = piece.astype(BF16)

    def plain(ref):
        part = tm // IN_PARTS
        for c in range(IN_PARTS):
            rows = slice(c * part, (c + 1) * part)
            ref[rows, :] = jnp.dot(h_ref[rows, :], w_ref[...], preferred_element_type=F32).astype(ref.dtype)

    @pl.when((j >= NB_ATT) & (j < NB_ATT + NB_MQK))
    def _():
        plain(mqk_ref)

    @pl.when((j >= NB_ATT + NB_MQK) & (j < NB_ATT + NB_MQK + NB_MV))
    def _():
        plain(mv_ref)

    @pl.when(j >= NB_ATT + NB_MQK + NB_MV)
    def _():
        plain(mo_ref)


def _in_proj(x, norm_w, w_main, w_gates_t, b_gates, *, seq, tm=1024):
    n, d = x.shape
    tn = IN_TN
    nb = NB_ATT + NB_MQK + NB_MV + NB_MO
    o1, o2, o3 = NB_ATT, NB_ATT + NB_MQK, NB_ATT + NB_MQK + NB_MV
    batch, tpb = n // seq, seq // tm

    def col(off, cnt):
        return lambda i, j: (i, jnp.clip(j - off, 0, cnt - 1))

    def att_spec(dil):
        return pl.BlockSpec((1, dil, tm // dil, tn),
                            lambda i, j: (i // tpb, 0, i % tpb, jnp.minimum(j, NB_ATT - 1)))

    def att_shape(dil):
        return jax.ShapeDtypeStruct((batch, dil, seq // dil, 3 * ATT_WIDTH), BF16)

    est = (2 * _nbytes((tm, d), F32) + _nbytes((tm, d), BF16) + 2 * _nbytes((d, tn), BF16)
           + 2 * (2 * _nbytes((tm, tn), F32) + 4 * _nbytes((tm, tn), BF16)) + 3 * _nbytes((tm, tn), F32))
    return pl.pallas_call(
        _in_proj_kernel,
        grid=(n // tm, nb),
        in_specs=[
            pl.BlockSpec((tm, d), lambda i, j: (i, 0)),
            pl.BlockSpec((1, d), lambda i, j: (0, 0)),
            pl.BlockSpec((d, tn), lambda i, j: (0, j)),
            pl.BlockSpec((N_GATES, d), lambda i, j: (0, 0)),
            pl.BlockSpec((N_GATES, 1), lambda i, j: (0, 0)),
        ],
        out_specs=[
            att_spec(DILATIONS[0]),
            att_spec(DILATIONS[1]),
            att_spec(DILATIONS[2]),
            pl.BlockSpec((tm, tn), col(o1, NB_MQK)),
            pl.BlockSpec((tm, tn), col(o2, NB_MV)),
            pl.BlockSpec((tm, tn), col(o3, NB_MO)),
            pl.BlockSpec((N_GATES, tm), lambda i, j: (0, i)),
        ],
        out_shape=[
            att_shape(DILATIONS[0]),
            att_shape(DILATIONS[1]),
            att_shape(DILATIONS[2]),
            jax.ShapeDtypeStruct((n, 2 * MLSTM_WIDTH), F32),
            jax.ShapeDtypeStruct((n, MLSTM_WIDTH), BF16),
            jax.ShapeDtypeStruct((n, MLSTM_WIDTH), F32),
            jax.ShapeDtypeStruct((N_GATES, n), F32),
        ],
        scratch_shapes=[pltpu.VMEM((tm, d), BF16), pltpu.VMEM((tn // LANES, tm, LANES), F32)],
        compiler_params=_params(("parallel", "arbitrary"), est),
        name="in_proj",
    )(x, norm_w.reshape(1, d), w_main, w_gates_t, b_gates.reshape(N_GATES, 1))


def _rel_bucket(rel):
    nb = N_BUCKETS // 2
    max_exact = nb // 2
    n = np.abs(rel)
    large = max_exact + (np.log(np.maximum(n, 1) / max_exact) / math.log(MAX_DISTANCE / max_exact)
                         * (nb - max_exact)).astype(np.int32)
    large = np.minimum(large, nb - 1)
    return np.where(rel > 0, nb, 0) + np.where(n < max_exact, n, large)


def _branch_bias(rel_table, dilation):
    key = np.arange(K_BLOCK)[None, :]
    off = key - HALF_WINDOW - np.arange(Q_BLOCK)[:, None]
    band = np.abs(off) <= HALF_WINDOW
    keep = np.stack([band & ((v & 1 == 0) | (key >= HALF_WINDOW)) & ((v & 2 == 0) | (key < K_BLOCK - HALF_WINDOW))
                     for v in range(4)])
    onehot = (_rel_bucket(off * dilation).reshape(1, -1) == np.arange(N_BUCKETS)[:, None]).astype(np.float32)
    bias = jnp.einsum("bh,bn->hn", rel_table.astype(F32) * LOG2_E, jnp.asarray(onehot),
                      precision=lax.Precision.HIGHEST)
    return jnp.where(keep[:, None], bias.reshape(1, ATT_HEADS, Q_BLOCK, K_BLOCK), NEG)


def _attn_branch_kernel(q_ref, k_ref, v_ref, bias_ref, o_ref, st_ref, kpad, vpad, *, seq, width):
    pw = pl.program_id(2)
    pairs = width // LANES

    zeros = jnp.zeros((HALF_WINDOW, width), BF16)
    kpad[0:HALF_WINDOW, :] = zeros
    vpad[0:HALF_WINDOW, :] = zeros
    kpad[HALF_WINDOW + seq:2 * HALF_WINDOW + seq, :] = zeros
    vpad[HALF_WINDOW + seq:2 * HALF_WINDOW + seq, :] = zeros
    kpad[HALF_WINDOW:HALF_WINDOW + seq, :] = k_ref[0, 0]
    vpad[HALF_WINDOW:HALF_WINDOW + seq, :] = v_ref[0, 0]

    @pl.when(pw == 0)
    def _():
        st_ref[...] = jnp.zeros_like(st_ref)

    lane = lax.broadcasted_iota(jnp.int32, (1, LANES), 1)
    low = lane < ATT_HEAD_DIM
    n_blocks = seq // Q_BLOCK
    unroll = math.gcd(n_blocks, ATT_UNROLL)

    for p in range(pairs):
        sl = slice(p * LANES, (p + 1) * LANES)
        head_pair = pw * pairs + p

        def block(q0, sl=sl, p=p, head_pair=head_pair):
            qb = q_ref[0, 0, pl.ds(q0, Q_BLOCK), sl]
            kw = kpad[pl.ds(q0, K_BLOCK), sl]
            vw = vpad[pl.ds(q0, K_BLOCK), sl]
            if isinstance(q0, int):
                var = int(q0 == 0) + 2 * int(q0 == seq - Q_BLOCK)
            else:
                var = jnp.where(q0 == 0, 1, 0) + jnp.where(q0 == seq - Q_BLOCK, 2, 0)
            zero = jnp.zeros_like(qb)
            qs = jnp.concatenate([jnp.where(low, qb, zero), jnp.where(low, zero, qb)], axis=0)
            s = lax.dot_general(qs, kw, NT_DIMS, preferred_element_type=F32)
            s0 = s[:Q_BLOCK] + bias_ref[var, 2 * p]
            s1 = s[Q_BLOCK:] + bias_ref[var, 2 * p + 1]
            m0 = jnp.max(s0, axis=1, keepdims=True)
            m1 = jnp.max(s1, axis=1, keepdims=True)
            e0 = jnp.exp2(s0 - m0)
            e1 = jnp.exp2(s1 - m1)
            l0 = jnp.sum(e0, axis=1, keepdims=True)
            l1 = jnp.sum(e1, axis=1, keepdims=True)
            e = jnp.concatenate([e0, e1], axis=0).astype(BF16)
            pv = jnp.dot(e, vw, preferred_element_type=F32)
            o_ref[0, 0, pl.ds(q0, Q_BLOCK), sl] = jnp.where(low, pv[:Q_BLOCK], pv[Q_BLOCK:])
            h0 = 2 * head_pair
            prev = st_ref[0, 0, pl.ds(q0, Q_BLOCK), :]
            st_ref[0, 0, pl.ds(q0, Q_BLOCK), :] = jnp.where(
                lane == h0, m0, jnp.where(
                    lane == h0 + 1, m1, jnp.where(
                        lane == ATT_HEADS + h0, l0, jnp.where(lane == ATT_HEADS + h0 + 1, l1, prev))))

        if n_blocks < ATT_UNROLL:
            for i in range(n_blocks):
                block(i * Q_BLOCK)
            continue

        def body(i, carry, block=block):
            for u in range(unroll):
                block(pl.multiple_of((i * unroll + u) * Q_BLOCK, Q_BLOCK))
            return carry

        lax.fori_loop(0, n_blocks // unroll, body, 0)


def _attn_width(seq):
    return int(min(ATT_WIDTH, max(LANES, (1 << 20) // seq // LANES * LANES)))


def _attn_branch(att, bias):
    b, d, seq, _ = att.shape
    width = _attn_width(seq)
    npw = ATT_WIDTH // width
    est = (2 * 3 * _nbytes((seq, width), BF16) + 2 * _nbytes((4, width // ATT_HEAD_DIM, Q_BLOCK, K_BLOCK), F32)
           + 2 * _nbytes((seq, width), F32) + 2 * _nbytes((seq, LANES), F32)
           + 2 * _nbytes((seq + 2 * HALF_WINDOW, width), BF16))
    kern = functools.partial(_attn_branch_kernel, seq=seq, width=width)
    return pl.pallas_call(
        kern,
        grid=(b, d, npw),
        in_specs=[
            pl.BlockSpec((1, 1, seq, width), lambda bi, r, pw: (bi, r, 0, pw)),
            pl.BlockSpec((1, 1, seq, width), lambda bi, r, pw: (bi, r, 0, npw + pw)),
            pl.BlockSpec((1, 1, seq, width), lambda bi, r, pw: (bi, r, 0, 2 * npw + pw)),
            pl.BlockSpec((4, width // ATT_HEAD_DIM, Q_BLOCK, K_BLOCK), lambda bi, r, pw: (0, pw, 0, 0)),
        ],
        out_specs=[
            pl.BlockSpec((1, 1, seq, width), lambda bi, r, pw: (bi, r, 0, pw)),
            pl.BlockSpec((1, 1, seq, LANES), lambda bi, r, pw: (bi, r, 0, 0)),
        ],
        out_shape=[
            jax.ShapeDtypeStruct((b, d, seq, ATT_WIDTH), F32),
            jax.ShapeDtypeStruct((b, d, seq, LANES), F32),
        ],
        scratch_shapes=[pltpu.VMEM((seq + 2 * HALF_WINDOW, width), BF16),
                        pltpu.VMEM((seq + 2 * HALF_WINDOW, width), BF16)],
        compiler_params=_params(("parallel", "parallel", "arbitrary"), est),
        name=f"attn_d{d}",
    )(att, att, att, bias)


def _attn_merge_kernel(n1_ref, n4_ref, n16_ref, s1_ref, s4_ref, s16_ref, ex_ref, nw_ref, y_ref,
                       t4_ref, t16_ref, out_ref):
    tm = y_ref.shape[0]
    lane = lax.broadcasted_iota(jnp.int32, (1, LANES), 1)
    head = lane < ATT_HEADS

    def natural(ref, tmp_ref, s):
        d = ref.shape[1]
        if d == 1:
            return ref[0, 0, :, s * LANES:(s + 1) * LANES]
        for r in range(d):
            tmp_ref[pl.ds(r, tm // d, stride=d), :] = ref[0, r, :, s * LANES:(s + 1) * LANES]
        return tmp_ref[...]

    st = [natural(s1_ref, None, 0), natural(s4_ref, t4_ref, 0), natural(s16_ref, t16_ref, 0)]
    m = jnp.maximum(jnp.maximum(st[0], st[1]), st[2])
    scale = [jnp.exp2(x - m) for x in st]
    den = sum(sc * pltpu.roll(x, LANES - ATT_HEADS, axis=1) for sc, x in zip(scale, st))
    inv = 1.0 / den
    ex = ex_ref[...]

    def spread(w):
        w = jnp.where(head, w, 0.0)
        hi = w.astype(BF16)
        lo = (w - hi.astype(F32)).astype(BF16)
        return (jnp.dot(hi, ex, preferred_element_type=F32) + jnp.dot(lo, ex, preferred_element_type=F32))

    wide = [spread(sc * inv) for sc in scale]
    refs = ((n1_ref, None), (n4_ref, t4_ref), (n16_ref, t16_ref))
    for s in range(ATT_WIDTH // LANES):
        sl = slice(s * LANES, (s + 1) * LANES)
        out_ref[:, sl] = sum(w[:, sl] * natural(ref, tmp, s) for w, (ref, tmp) in zip(wide, refs))
    y_ref[...] = _rms(out_ref[...], nw_ref[...]).astype(BF16)


def _attn_merge(nums, stats, norm_w, *, tm=512):
    b, _, s, _ = nums[0].shape
    n, tpb = b * s, s // tm
    expand = np.zeros((LANES, ATT_WIDTH), np.float32)
    for h in range(ATT_HEADS):
        expand[h, h * ATT_HEAD_DIM:(h + 1) * ATT_HEAD_DIM] = 1.0
    est = (2 * 3 * (_nbytes((tm, ATT_WIDTH), F32) + _nbytes((tm, LANES), F32)) + 5 * _nbytes((tm, ATT_WIDTH), F32)
           + 2 * _nbytes((LANES, ATT_WIDTH), F32))

    def blk(d, c):
        return pl.BlockSpec((1, d, tm // d, c), lambda i: (i // tpb, 0, i % tpb, 0))

    fixed = lambda i: (0, 0)
    return pl.pallas_call(
        _attn_merge_kernel,
        grid=(n // tm,),
        in_specs=[blk(d, ATT_WIDTH) for d in DILATIONS] + [blk(d, LANES) for d in DILATIONS] + [
            pl.BlockSpec((LANES, ATT_WIDTH), fixed), pl.BlockSpec((1, ATT_WIDTH), fixed)],
        out_specs=pl.BlockSpec((tm, ATT_WIDTH), lambda i: (i, 0)),
        out_shape=jax.ShapeDtypeStruct((n, ATT_WIDTH), BF16),
        scratch_shapes=[pltpu.VMEM((tm, LANES), F32), pltpu.VMEM((tm, LANES), F32),
                        pltpu.VMEM((tm, ATT_WIDTH), F32)],
        compiler_params=_params(("parallel",), est),
        name="attn_merge",
    )(*nums, *stats, jnp.asarray(expand, dtype=BF16), norm_w.reshape(1, ATT_WIDTH))


CONV_ROWS = 512
CONV_COLS = 256


def _conv_silu_kernel(x_ref, w_ref, b_ref, y_ref, *, seq):
    cb = pl.program_id(1)
    scale = jnp.where(cb >= MLSTM_WIDTH // CONV_COLS, MLSTM_HEAD_DIM ** -0.5, 1.0).astype(F32)
    w0, w1, w2 = w_ref[0:1, :], w_ref[1:2, :], w_ref[2:3, :]
    bias = b_ref[...]
    rows = lax.broadcasted_iota(jnp.int32, (CONV_ROWS, 1), 0)

    def body(c, carry):
        c0 = pl.multiple_of(c * CONV_ROWS, CONV_ROWS)
        xc = x_ref[0, pl.ds(c0, CONV_ROWS), :]
        before = x_ref[0, pl.ds(pl.multiple_of(jnp.maximum(c0 - 8, 0), 8), 8), :][7:8, :]
        after = x_ref[0, pl.ds(pl.multiple_of(jnp.minimum(c0 + CONV_ROWS, seq - 8), 8), 8), :][0:1, :]
        before = jnp.where(c0 > 0, before, 0.0)
        after = jnp.where(c0 + CONV_ROWS < seq, after, 0.0)
        up = jnp.where(rows == 0, before, pltpu.roll(xc, 1, axis=0))
        dn = jnp.where(rows == CONV_ROWS - 1, after, pltpu.roll(xc, CONV_ROWS - 1, axis=0))
        y = w0 * up + w1 * xc + w2 * dn + bias
        y_ref[0, pl.ds(c0, CONV_ROWS), :] = (y * jax.nn.sigmoid(y) * scale).astype(BF16)
        return carry

    lax.fori_loop(0, seq // CONV_ROWS, body, 0)


def _conv_silu(x, conv_w, conv_b):
    b, s, c = x.shape
    est = 2 * _nbytes((s, CONV_COLS), F32) + 2 * _nbytes((s, CONV_COLS), BF16) + 8 * _nbytes((CONV_ROWS, CONV_COLS), F32)
    return pl.pallas_call(
        functools.partial(_conv_silu_kernel, seq=s),
        grid=(b, c // CONV_COLS),
        in_specs=[
            pl.BlockSpec((1, s, CONV_COLS), lambda bi, cb: (bi, 0, cb)),
            pl.BlockSpec((3, CONV_COLS), lambda bi, cb: (0, cb)),
            pl.BlockSpec((1, CONV_COLS), lambda bi, cb: (0, cb)),
        ],
        out_specs=pl.BlockSpec((1, s, CONV_COLS), lambda bi, cb: (bi, 0, cb)),
        out_shape=jax.ShapeDtypeStruct((b, s, c), BF16),
        compiler_params=_params(("parallel", "parallel"), est),
        name="conv_silu",
    )(x, conv_w, conv_b.reshape(1, c))


EXT = MLSTM_HEAD_DIM + LANES


def _log_sigmoid(x):
    return jnp.minimum(x, 0.0) - jnp.log1p(jnp.exp(-jnp.abs(x)))


def _mlstm_kernel(*refs, reverse, tile):
    if reverse:
        q_ref, k_ref, v_ref, gt_ref, hf_ref, o_ref, nw_ref, out_ref, cext_ref, m_ref, hb_ref = refs
    else:
        q_ref, k_ref, v_ref, gt_ref, out_ref, cext_ref, m_ref = refs
    C = MLSTM_CHUNK[reverse]
    nchunks = tile // C

    @pl.when(pl.program_id(1) == 0)
    def _():
        cext_ref[...] = jnp.zeros_like(cext_ref)
        m_ref[...] = jnp.full_like(m_ref, NEG)

    r_i = lax.broadcasted_iota(jnp.int32, (C, C), 0)
    c_i = lax.broadcasted_iota(jnp.int32, (C, C), 1)
    if reverse:
        causal, causal_t = c_i >= r_i, r_i >= c_i
    else:
        causal, causal_t = c_i <= r_i, r_i <= c_i
    tri_row = jnp.where(causal_t, 1.0, 0.0).astype(F32)
    tri_col = jnp.where(causal, 1.0, 0.0).astype(F32)
    eye = jnp.where(r_i == c_i, 1.0, 0.0).astype(F32)
    ones_lane = jnp.ones((C, LANES), BF16)
    hi = lax.Precision.HIGHEST
    g_i = 2 * MLSTM_HEADS if reverse else 0
    g_f = g_i + MLSTM_HEADS

    def chunk(ci):
        c = (nchunks - 1 - ci) if reverse else ci
        t0 = pl.multiple_of(c * C, C)
        gates = gt_ref[:, pl.ds(t0, C)]
        lsig = _log_sigmoid(gates)
        cum_rows = jnp.dot(lsig, tri_row, preferred_element_type=F32, precision=hi)
        cum_cols = lax.dot_general(tri_col, lsig, NT_DIMS, preferred_element_type=F32, precision=hi)
        gate_cols = lax.dot_general(eye, gates, NT_DIMS, preferred_element_type=F32, precision=hi)
        li, li_cols = gates[g_i:g_i + MLSTM_HEADS], gate_cols[:, g_i:g_i + MLSTM_HEADS]
        b_rows, b_cols = cum_rows[g_f:g_f + MLSTM_HEADS], cum_cols[:, g_f:g_f + MLSTM_HEADS]
        for hd in range(MLSTM_HEADS):
            hs = slice(hd * MLSTM_HEAD_DIM, (hd + 1) * MLSTM_HEAD_DIM)
            q = q_ref[0, pl.ds(t0, C), hs]
            k = k_ref[0, pl.ds(t0, C), hs]
            v = v_ref[0, pl.ds(t0, C), hs]
            vext = jnp.concatenate([v, ones_lane], axis=1)
            bc, br = b_cols[:, hd:hd + 1], b_rows[hd:hd + 1, :]
            lir, lic = li[hd:hd + 1, :], li_cols[:, hd:hd + 1]
            m_prev = m_ref[hd, 0:1, 0:1]
            dm = jnp.where(causal, bc - br + lir, -jnp.inf)
            inter = bc + m_prev
            m_t = jnp.maximum(inter, jnp.max(dm, axis=1, keepdims=True))
            sqk = lax.dot_general(q, k, NT_DIMS, preferred_element_type=F32)
            w_intra = (jnp.exp(dm - m_t) * sqk).astype(BF16)
            w_inter = jnp.exp(inter - m_t)
            cext = cext_ref[hd]
            num = (jnp.dot(w_intra, vext, preferred_element_type=F32)
                   + w_inter * jnp.dot(q, cext.astype(BF16), preferred_element_type=F32))
            den =jnp.concatenate([num[:, MLSTM_HEAD_DIM:]] * (MLSTM_HEAD_DIM // LANES), axis=1)
            h = num[:, :MLSTM_HEAD_DIM] / jnp.maximum(jnp.abs(den), jnp.exp(-m_t))
            g = br[:, 0:1] if reverse else br[:, C - 1:C]
            m_new = jnp.maximum(g + m_prev, jnp.max(g - br + lir, axis=1, keepdims=True))
            w_a = jnp.exp(g - bc + lic - m_new)
            decay = jnp.exp(g + m_prev - m_new)
            vw = (vext.astype(F32) * w_a).astype(BF16)
            cext_ref[hd] = decay * cext + lax.dot_general(k, vw, TN_DIMS, preferred_element_type=F32)
            m_ref[hd] = jnp.broadcast_to(m_new, m_ref.shape[1:])
            if reverse:
                hb_ref[pl.ds(t0, C), hs] = h
            else:
                out_ref[0, pl.ds(t0, C), hs] = h

    def body(i, carry):
        for u in range(MLSTM_UNROLL):
            chunk(i * MLSTM_UNROLL + u)
        return carry

    lax.fori_loop(0, nchunks // MLSTM_UNROLL, body, 0)

    if reverse:
        def norm(i, carry):
            rows = pl.ds(pl.multiple_of(i * NORM_ROWS, NORM_ROWS), NORM_ROWS)
            for hd in range(MLSTM_HEADS):
                hs = slice(hd * MLSTM_HEAD_DIM, (hd + 1) * MLSTM_HEAD_DIM)
                hsum = hb_ref[rows, hs] + hf_ref[0, rows, hs]
                mu = jnp.mean(hsum, axis=1, keepdims=True)
                dev = hsum - mu
                var = jnp.mean(dev * dev, axis=1, keepdims=True)
                y = dev * lax.rsqrt(var + EPS) * nw_ref[:, hs] * jax.nn.sigmoid(o_ref[0, rows, hs])
                out_ref[0, rows, hs] = y.astype(BF16)
            return carry

        lax.fori_loop(0, tile // NORM_ROWS, norm, 0)


def _mlstm(qk, v, gates_t, o_pre, norm_w, *, tile=512):
    b, s, _ = v.shape
    ns = s // tile
    w = MLSTM_WIDTH
    scratch = [pltpu.VMEM((MLSTM_HEADS, MLSTM_HEAD_DIM, EXT), F32), pltpu.VMEM((MLSTM_HEADS, 8, LANES), F32)]
    est = (2 * 3 * _nbytes((tile, w), BF16) + 2 * 3 * _nbytes((tile, w), F32)
           + _nbytes((MLSTM_HEADS, MLSTM_HEAD_DIM, EXT), F32) + 16 * _nbytes((max(MLSTM_CHUNK.values()), EXT), F32))

    def specs(order):
        return [
            pl.BlockSpec((1, tile, w), lambda bi, si: (bi, order(si), 0)),
            pl.BlockSpec((1, tile, w), lambda bi, si: (bi, order(si), 1)),
            pl.BlockSpec((1, tile, w), lambda bi, si: (bi, order(si), 0)),
            pl.BlockSpec((N_GATES, tile), lambda bi, si: (0, bi * ns + order(si))),
        ]

    fwd = lambda si: si
    bwd = lambda si: ns - 1 - si
    h_fwd = pl.pallas_call(
        functools.partial(_mlstm_kernel, reverse=False, tile=tile),
        grid=(b, ns),
        in_specs=specs(fwd),
        out_specs=pl.BlockSpec((1, tile, w), lambda bi, si: (bi, si, 0)),
        out_shape=jax.ShapeDtypeStruct((b, s, w), F32),
        scratch_shapes=scratch,
        compiler_params=_params(("parallel", "arbitrary"), est),
        name="mlstm_fwd",
    )(qk, qk, v, gates_t)
    return pl.pallas_call(
        functools.partial(_mlstm_kernel, reverse=True, tile=tile),
        grid=(b, ns),
        in_specs=specs(bwd) + [
            pl.BlockSpec((1, tile, w), lambda bi, si: (bi, bwd(si), 0)),
            pl.BlockSpec((1, tile, w), lambda bi, si: (bi, bwd(si), 0)),
            pl.BlockSpec((1, w), lambda bi, si: (0, 0)),
        ],
        out_specs=pl.BlockSpec((1, tile, w), lambda bi, si: (bi, bwd(si), 0)),
        out_shape=jax.ShapeDtypeStruct((b, s, w), BF16),
        scratch_shapes=scratch + [pltpu.VMEM((tile, w), F32)],
        compiler_params=_params(("parallel", "arbitrary"), est + _nbytes((tile, w), F32)),
        name="mlstm_bwd",
    )(qk, qk, v, gates_t, h_fwd, o_pre, norm_w.reshape(1, w))


def _out_proj_kernel(x_ref, ya_ref, ym_ref, wa_ref, wm_ref, o_ref):
    o_ref[...] = (x_ref[...]
                  + jnp.dot(ya_ref[...], wa_ref[...], preferred_element_type=F32)
                  + jnp.dot(ym_ref[...], wm_ref[...], preferred_element_type=F32))


def _out_proj(x, y_att, y_mem, w_out, *, tm=512):
    n, d = x.shape
    est = (2 * 2 * _nbytes((tm, d), F32) + 2 * 2 * _nbytes((tm, ATT_WIDTH), BF16)
           + 2 * _nbytes((d, d), BF16) + 2 * _nbytes((tm, d), F32))
    row = lambda i: (i, 0)
    return pl.pallas_call(
        _out_proj_kernel,
        grid=(n // tm,),
        in_specs=[
            pl.BlockSpec((tm, d), row),
            pl.BlockSpec((tm, ATT_WIDTH), row),
            pl.BlockSpec((tm, MLSTM_WIDTH), row),
            pl.BlockSpec((ATT_WIDTH, d), lambda i: (0, 0)),
            pl.BlockSpec((MLSTM_WIDTH, d), lambda i: (1, 0)),
        ],
        out_specs=pl.BlockSpec((tm, d), row),
        out_shape=jax.ShapeDtypeStruct((n, d), F32),
        compiler_params=_params(("parallel",), est),
        name="out_proj",
    )(x, y_att, y_mem, w_out, w_out)


def _ple_final_kernel(x_ref, p_ref, nw_ref, wg_ref, wp_ref, fw_ref, o_ref):
    x = x_ref[...]
    h = _rms(x, nw_ref[...]).astype(BF16)
    gate = jax.nn.sigmoid(jnp.dot(h, wg_ref[...], preferred_element_type=F32))
    proj = jnp.dot(p_ref[...].astype(BF16), wp_ref[...], preferred_element_type=F32)
    o_ref[...] = _rms(x + gate * proj, fw_ref[...])


def _ple_final(x, p, norm_w, w_gate, w_proj, final_w, *, tm=512):
    n, d = x.shape
    est = (2 * 2 * _nbytes((tm, d), F32) + 2 * _nbytes((tm, PLE_DIM), F32) + 2 * _nbytes((d, d), BF16)
           + 2 * _nbytes((PLE_DIM, d), BF16) + 4 * _nbytes((tm, d), F32))
    row = lambda i: (i, 0)
    fixed = lambda i: (0, 0)
    return pl.pallas_call(
        _ple_final_kernel,
        grid=(n // tm,),
        in_specs=[
            pl.BlockSpec((tm, d), row),
            pl.BlockSpec((tm, PLE_DIM), row),
            pl.BlockSpec((1, d), fixed),
            pl.BlockSpec((d, d), fixed),
            pl.BlockSpec((PLE_DIM, d), fixed),
            pl.BlockSpec((1, d), fixed),
        ],
        out_specs=pl.BlockSpec((tm, d), row),
        out_shape=jax.ShapeDtypeStruct((n, d), F32),
        compiler_params=_params(("parallel",), est),
        name="ple_final",
    )(x, p, norm_w.reshape(1, d), w_gate, w_proj, final_w.reshape(1, d))


def _trunk(x, p, wts):
    b, s, d = x.shape
    n = b * s
    h = _ffn(x.reshape(n, d), wts["ffn1_norm"], wts["ffn1_w_in"], wts["ffn1_w_out"])
    *atts, mqk, mv, mo, gates_t = _in_proj(h, wts["mix_norm"], wts["w_in_main"], wts["w_in_gates_t"],
                                           wts["b_gates"], seq=s)
    nums, stats = zip(*[_attn_branch(att, wts["bias"][d_]) for att, d_ in zip(atts, DILATIONS)])
    y_att = _attn_merge(nums, stats, wts["attn_out_norm"])
    qk = _conv_silu(mqk.reshape(b, s, 2 * MLSTM_WIDTH), wts["conv_w"], wts["conv_b"])
    y_mem = _mlstm(qk, mv.reshape(b, s, MLSTM_WIDTH), gates_t, mo.reshape(b, s, MLSTM_WIDTH),
                   wts["mlstm_out_norm"])
    h = _out_proj(h, y_att, y_mem.reshape(n, MLSTM_WIDTH), wts["w_out"])
    h = _ffn(h, wts["ffn2_norm"], wts["ffn2_w_in"], wts["ffn2_w_out"])
    y = _ple_final(h, p.reshape(n, PLE_DIM), wts["ple_norm"], wts["ple_w_gate"], wts["ple_w_proj"],
                   wts["final_norm"])
    return y.reshape(b, s, d)


def kernel(x_prompt, x_sample, p_prompt, p_sample, rel_table, ffn1_norm, ffn1_w_in, ffn1_w_out, mix_norm, w_in, b_gates, conv_w, conv_b, attn_out_norm, mlstm_out_norm, w_out, ffn2_norm, ffn2_w_in, ffn2_w_out, ple_norm, ple_w_gate, ple_w_proj, final_norm):
    depth = ffn1_norm.shape[0]
    assert depth == 1
    i = 0
    n_main = 3 * ATT_WIDTH + 4 * MLSTM_WIDTH
    col_scale = jnp.concatenate([jnp.full((ATT_WIDTH,), ATT_HEAD_DIM ** -0.5 * LOG2_E, F32),
                                 jnp.ones((n_main - ATT_WIDTH,), F32)])
    wts = dict(
        ffn1_norm=ffn1_norm[i], ffn1_w_in=ffn1_w_in[i].astype(BF16), ffn1_w_out=ffn1_w_out[i].astype(BF16),
        mix_norm=mix_norm[i],
        w_in_main=(w_in[i][:, :n_main] * col_scale).astype(BF16),
        w_in_gates_t=w_in[i][:, n_main:].T.astype(BF16),
        b_gates=b_gates[i], conv_w=conv_w[i], conv_b=conv_b[i],
        attn_out_norm=attn_out_norm[i], mlstm_out_norm=mlstm_out_norm[i],
        w_out=w_out[i].astype(BF16),
        ffn2_norm=ffn2_norm[i], ffn2_w_in=ffn2_w_in[i].astype(BF16), ffn2_w_out=ffn2_w_out[i].astype(BF16),
        ple_norm=ple_norm[i], ple_w_gate=ple_w_gate[i].astype(BF16), ple_w_proj=ple_w_proj[i].astype(BF16),
        final_norm=final_norm,
        bias={d_: _branch_bias(rel_table, d_) for d_ in DILATIONS},
    )
    return (_trunk(x_prompt, p_prompt[i], wts), _trunk(x_sample, p_sample[i], wts))
```

```python
import functools
import math

import jax
import jax.numpy as jnp
import numpy as np
from jax import lax
from jax.experimental import pallas as pl
from jax.experimental.pallas import tpu as pltpu

F32 = jnp.float32
BF16 = jnp.bfloat16

D_MODEL = 2048
ATT_HEADS = 16
ATT_HEAD_DIM = 64
ATT_WIDTH = ATT_HEADS * ATT_HEAD_DIM
DILATIONS = (1, 4, 16)
HALF_WINDOW = 64
N_BUCKETS = 32
MAX_DISTANCE = 1024
MLSTM_HEADS = 4
MLSTM_HEAD_DIM = 256
MLSTM_WIDTH = MLSTM_HEADS * MLSTM_HEAD_DIM
N_GATES = 4 * MLSTM_HEADS
D_FF = 5632
PLE_DIM = 256
EPS = 1e-6
NEG = -1e30
LOG2_E = math.log2(math.e)

LANES = 128
V7X_VMEM_BYTES = 64 * 1024 * 1024
VMEM_CAP = V7X_VMEM_BYTES - 6 * 1024 * 1024
VMEM_SLACK = 8 * 1024 * 1024

Q_BLOCK = 128
K_BLOCK = Q_BLOCK + 2 * HALF_WINDOW
ATT_MIN_ROWS = 512
ATT_UNROLL = 16
MLSTM_CHUNK = {False: 256, True: 256}
NORM_ROWS = 128
MLSTM_UNROLL = 1

NT_DIMS = (((1,), (1,)), ((), ()))
TN_DIMS = (((0,), (0,)), ((), ()))


def _params(semantics, est_bytes):
    limit = int(min(est_bytes + VMEM_SLACK, VMEM_CAP))
    return pltpu.CompilerParams(dimension_semantics=semantics, vmem_limit_bytes=limit)


def _nbytes(shape, dtype):
    return math.prod(shape) * jnp.dtype(dtype).itemsize


def _rms(x, w):
    ms = jnp.mean(x * x, axis=-1, keepdims=True)
    return x * lax.rsqrt(ms + EPS) * w


def _ffn_kernel(x_ref, nw_ref, wg_ref, wu_ref, wo_ref, o_ref, h_ref):
    f = pl.program_id(1)

    @pl.when(f == 0)
    def _():
        x = x_ref[...]
        h_ref[...] = _rms(x, nw_ref[...]).astype(BF16)
        o_ref[...] = x

    h = h_ref[...]
    g = jnp.dot(h, wg_ref[...], preferred_element_type=F32)
    u = jnp.dot(h, wu_ref[...], preferred_element_type=F32)
    a = (g * jax.nn.sigmoid(g) * u * 0.5).astype(BF16)
    o_ref[...] += jnp.dot(a, wo_ref[...], preferred_element_type=F32)


def _ffn(x, norm_w, w_in, w_out, *, tm=1024, tf=512):
    n, d = x.shape
    nf = D_FF // tf
    est = (2 * 2 * _nbytes((tm, d), F32) + _nbytes((tm, d), BF16)
           + 2 * 3 * _nbytes((d, tf), BF16) + 3 * _nbytes((tm, tf), F32))
    return pl.pallas_call(
        _ffn_kernel,
        grid=(n // tm, nf),
        in_specs=[
            pl.BlockSpec((tm, d), lambda i, f: (i, 0)),
            pl.BlockSpec((1, d), lambda i, f: (0, 0)),
            pl.BlockSpec((d, tf), lambda i, f: (0, f)),
            pl.BlockSpec((d, tf), lambda i, f: (0, f + nf)),
            pl.BlockSpec((tf, d), lambda i, f: (f, 0)),
        ],
        out_specs=pl.BlockSpec((tm, d), lambda i, f: (i, 0)),
        out_shape=jax.ShapeDtypeStruct((n, d), F32),
        scratch_shapes=[pltpu.VMEM((tm, d), BF16)],
        compiler_params=_params(("parallel", "arbitrary"), est),
        name="ffn",
    )(x, norm_w.reshape(1, d), w_in, w_in, w_out)


IN_TN = 512
NB_ATT = 3 * ATT_WIDTH // IN_TN
NB_MQK = 2 * MLSTM_WIDTH // IN_TN
NB_MV = MLSTM_WIDTH // IN_TN
NB_MO = MLSTM_WIDTH // IN_TN
IN_PARTS = 4


def _in_proj_kernel(x_ref, nw_ref, w_ref, wgt_ref, bg_ref,
                    att_ref, att4_ref, att16_ref, mqk_ref, mv_ref, mo_ref, gt_ref, h_ref, zs_ref):
    j = pl.program_id(1)
    tm = x_ref.shape[0]

    @pl.when(j == 0)
    def _():
        h = _rms(x_ref[...], nw_ref[...]).astype(BF16)
        h_ref[...] = h
        gt_ref[...] = lax.dot_general(wgt_ref[...], h, NT_DIMS, preferred_element_type=F32) + bg_ref[...]

    @pl.when(j < NB_ATT)
    def _():
        part = tm // IN_PARTS
        for c in range(IN_PARTS):
            z = jnp.dot(h_ref[c * part:(c + 1) * part, :], w_ref[...], preferred_element_type=F32)
            att_ref[0, 0, c * part:(c + 1) * part, :] = z.astype(BF16)
            for s in range(IN_TN // LANES):
                zs_ref[s, c * part:(c + 1) * part, :] = z[:, s * LANES:(s + 1) * LANES]
            for d, ref in ((DILATIONS[1], att4_ref), (DILATIONS[2], att16_ref)):
                rows = part // d
                for r in range(d):
                    for s in range(IN_TN // LANES):
                        piece = zs_ref[s, pl.ds(c * part + r, rows, stride=d), :]
                        ref[0, r, c * rows:(c + 1) * rows, s * LANES:(s + 1) * LANES] = piece.astype(BF16)

    def plain(ref):
        part = tm // IN_PARTS
        for c in range(IN_PARTS):
            rows = slice(c * part, (c + 1) * part)
            ref[rows, :] = jnp.dot(h_ref[rows, :], w_ref[...], preferred_element_type=F32).astype(ref.dtype)

    @pl.when((j >= NB_ATT) & (j < NB_ATT + NB_MQK))
    def _():
        plain(mqk_ref)

    @pl.when((j >= NB_ATT + NB_MQK) & (j < NB_ATT + NB_MQK + NB_MV))
    def _():
        plain(mv_ref)

    @pl.when(j >= NB_ATT + NB_MQK + NB_MV)
    def _():
        plain(mo_ref)


def _in_proj(x, norm_w, w_main, w_gates_t, b_gates, *, seq, tm=1024):
    n, d = x.shape
    tn = IN_TN
    nb = NB_ATT + NB_MQK + NB_MV + NB_MO
    o1, o2, o3 = NB_ATT, NB_ATT + NB_MQK, NB_ATT + NB_MQK + NB_MV
    batch, tpb = n // seq, seq // tm

    def col(off, cnt):
        return lambda i, j: (i, jnp.clip(j - off, 0, cnt - 1))

    def att_spec(dil):
        return pl.BlockSpec((1, dil, tm // dil, tn),
                            lambda i, j: (i // tpb, 0, i % tpb, jnp.minimum(j, NB_ATT - 1)))

    def att_shape(dil):
        return jax.ShapeDtypeStruct((batch, dil, seq // dil, 3 * ATT_WIDTH), BF16)

    est = (2 * _nbytes((tm, d), F32) + _nbytes((tm, d), BF16) + 2 * _nbytes((d, tn), BF16)
           + 2 * (2 * _nbytes((tm, tn), F32) + 4 * _nbytes((tm, tn), BF16)) + 3 * _nbytes((tm, tn), F32))
    return pl.pallas_call(
        _in_proj_kernel,
        grid=(n // tm, nb),
        in_specs=[
            pl.BlockSpec((tm, d), lambda i, j: (i, 0)),
            pl.BlockSpec((1, d), lambda i, j: (0, 0)),
            pl.BlockSpec((d, tn), lambda i, j: (0, j)),
            pl.BlockSpec((N_GATES, d), lambda i, j: (0, 0)),
            pl.BlockSpec((N_GATES, 1), lambda i, j: (0, 0)),
        ],
        out_specs=[
            att_spec(DILATIONS[0]),
            att_spec(DILATIONS[1]),
            att_spec(DILATIONS[2]),
            pl.BlockSpec((tm, tn), col(o1, NB_MQK)),
            pl.BlockSpec((tm, tn), col(o2, NB_MV)),
            pl.BlockSpec((tm, tn), col(o3, NB_MO)),
            pl.BlockSpec((N_GATES, tm), lambda i, j: (0, i)),
        ],
        out_shape=[
            att_shape(DILATIONS[0]),
            att_shape(DILATIONS[1]),
            att_shape(DILATIONS[2]),
            jax.ShapeDtypeStruct((n, 2 * MLSTM_WIDTH), F32),
            jax.ShapeDtypeStruct((n, MLSTM_WIDTH), BF16),
            jax.ShapeDtypeStruct((n, MLSTM_WIDTH), F32),
            jax.ShapeDtypeStruct((N_GATES, n), F32),
        ],
        scratch_shapes=[pltpu.VMEM((tm, d), BF16), pltpu.VMEM((tn // LANES, tm, LANES), F32)],
        compiler_params=_params(("parallel", "arbitrary"), est),
        name="in_proj",
    )(x, norm_w.reshape(1, d), w_main, w_gates_t, b_gates.reshape(N_GATES, 1))


def _rel_bucket(rel):
    nb = N_BUCKETS // 2
    max_exact = nb // 2
    n = np.abs(rel)
    large = max_exact + (np.log(np.maximum(n, 1) / max_exact) / math.log(MAX_DISTANCE / max_exact)
                         * (nb - max_exact)).astype(np.int32)
    large = np.minimum(large, nb - 1)
    return np.where(rel > 0, nb, 0) + np.where(n < max_exact, n, large)


def _branch_bias(rel_table, dilation):
    key = np.arange(K_BLOCK)[None, :]
    off = key - HALF_WINDOW - np.arange(Q_BLOCK)[:, None]
    band = np.abs(off) <= HALF_WINDOW
    keep = np.stack([band & ((v & 1 == 0) | (key >= HALF_WINDOW)) & ((v & 2 == 0) | (key < K_BLOCK - HALF_WINDOW))
                     for v in range(4)])
    onehot = (_rel_bucket(off * dilation).reshape(1, -1) == np.arange(N_BUCKETS)[:, None]).astype(np.float32)
    bias = jnp.einsum("bh,bn->hn", rel_table.astype(F32) * LOG2_E, jnp.asarray(onehot),
                      precision=lax.Precision.HIGHEST)
    return jnp.where(keep[:, None], bias.reshape(1, ATT_HEADS, Q_BLOCK, K_BLOCK), NEG)


def _attn_branch_kernel(q_ref, k_ref, v_ref, bias_ref, o_ref, st_ref, kpad, vpad, *, seq, width):
    pw = pl.program_id(2)
    pairs = width // LANES
    nres = q_ref.shape[1]

    zeros = jnp.zeros((HALF_WINDOW, width), BF16)
    for ri in range(nres):
        kpad[ri, 0:HALF_WINDOW, :] = zeros
        vpad[ri, 0:HALF_WINDOW, :] = zeros
        kpad[ri, HALF_WINDOW + seq:2 * HALF_WINDOW + seq, :] = zeros
        vpad[ri, HALF_WINDOW + seq:2 * HALF_WINDOW + seq, :] = zeros
        kpad[ri, HALF_WINDOW:HALF_WINDOW + seq, :] = k_ref[0, ri]
        vpad[ri, HALF_WINDOW:HALF_WINDOW + seq, :] = v_ref[0, ri]

    @pl.when(pw == 0)
    def _():
        st_ref[...] = jnp.zeros_like(st_ref)

    lane = lax.broadcasted_iota(jnp.int32, (1, LANES), 1)
    low = lane < ATT_HEAD_DIM
    n_blocks = seq // Q_BLOCK
    unroll = math.gcd(n_blocks, ATT_UNROLL)

    for ri, p in [(ri, p) for ri in range(nres) for p in range(pairs)]:
        sl = slice(p * LANES, (p + 1) * LANES)
        head_pair = pw * pairs + p

        def block(q0, sl=sl, p=p, ri=ri, head_pair=head_pair):
            qb = q_ref[0, ri, pl.ds(q0, Q_BLOCK), sl]
            kw = kpad[ri, pl.ds(q0, K_BLOCK), sl]
            vw = vpad[ri, pl.ds(q0, K_BLOCK), sl]
            if isinstance(q0, int):
                var = int(q0 == 0) + 2 * int(q0 == seq - Q_BLOCK)
            else:
                var = jnp.where(q0 == 0, 1, 0) + jnp.where(q0 == seq - Q_BLOCK, 2, 0)
            zero = jnp.zeros_like(qb)
            qs = jnp.concatenate([jnp.where(low, qb, zero), jnp.where(low, zero, qb)], axis=0)
            s = lax.dot_general(qs, kw, NT_DIMS, preferred_element_type=F32)
            s0 = s[:Q_BLOCK] + bias_ref[var, 2 * p]
            s1 = s[Q_BLOCK:] + bias_ref[var, 2 * p + 1]
            m0 = jnp.max(s0, axis=1, keepdims=True)
            m1 = jnp.max(s1, axis=1, keepdims=True)
            e0 = jnp.exp2(s0 - m0)
            e1 = jnp.exp2(s1 - m1)
            l0 = jnp.sum(e0, axis=1, keepdims=True)
            l1 = jnp.sum(e1, axis=1, keepdims=True)
            e = jnp.concatenate([e0, e1], axis=0).astype(BF16)
            pv = jnp.dot(e, vw, preferred_element_type=F32)
            o_ref[0, ri, pl.ds(q0, Q_BLOCK), sl] = jnp.where(low, pv[:Q_BLOCK], pv[Q_BLOCK:])
            h0 = 2 * head_pair
            prev = st_ref[0, ri, pl.ds(q0, Q_BLOCK), :]
            st_ref[0, ri, pl.ds(q0, Q_BLOCK), :] = jnp.where(
                lane == h0, m0, jnp.where(
                    lane == h0 + 1, m1, jnp.where(
                        lane == ATT_HEADS + h0, l0, jnp.where(lane == ATT_HEADS + h0 + 1, l1, prev))))

        if n_blocks < ATT_UNROLL:
            for i in range(n_blocks):
                block(i * Q_BLOCK)
            continue

        def body(i, carry, block=block):
            for u in range(unroll):
                block(pl.multiple_of((i * unroll + u) * Q_BLOCK, Q_BLOCK))
            return carry

        lax.fori_loop(0, n_blocks // unroll, body, 0)


def _attn_width(seq):
    return int(min(ATT_WIDTH, max(LANES, (1 << 20) // seq // LANES * LANES)))


def _attn_branch(att, bias):
    b, d, seq, _ = att.shape
    width = _attn_width(seq)
    npw = ATT_WIDTH // width
    nres = min(d, max(1, ATT_MIN_ROWS // seq))
    est = nres * (2 * 3 * _nbytes((seq, width), BF16) + 2 * _nbytes((seq, width), F32)
                  + 2 * _nbytes((seq, LANES), F32) + 2 * _nbytes((seq + 2 * HALF_WINDOW, width), BF16)
                  ) + 2 * _nbytes((4, width // ATT_HEAD_DIM, Q_BLOCK, K_BLOCK), F32)
    kern = functools.partial(_attn_branch_kernel, seq=seq, width=width)
    return pl.pallas_call(
        kern,
        grid=(b, d // nres, npw),
        in_specs=[
            pl.BlockSpec((1, nres, seq, width), lambda bi, r, pw: (bi, r, 0, pw)),
            pl.BlockSpec((1, nres, seq, width), lambda bi, r, pw: (bi, r, 0, npw + pw)),
            pl.BlockSpec((1, nres, seq, width), lambda bi, r, pw: (bi, r, 0, 2 * npw + pw)),
            pl.BlockSpec((4, width // ATT_HEAD_DIM, Q_BLOCK, K_BLOCK), lambda bi, r, pw: (0, pw, 0, 0)),
        ],
        out_specs=[
            pl.BlockSpec((1, nres, seq, width), lambda bi, r, pw: (bi, r, 0, pw)),
            pl.BlockSpec((1, nres, seq, LANES), lambda bi, r, pw: (bi, r, 0, 0)),
        ],
        out_shape=[
            jax.ShapeDtypeStruct((b, d, seq, ATT_WIDTH), F32),
            jax.ShapeDtypeStruct((b, d, seq, LANES), F32),
        ],
        scratch_shapes=[pltpu.VMEM((nres, seq + 2 * HALF_WINDOW, width), BF16),
                        pltpu.VMEM((nres, seq + 2 * HALF_WINDOW, width), BF16)],
        compiler_params=_params(("parallel", "parallel", "arbitrary"), est),
        name=f"attn_d{d}",
    )(att, att, att, bias)


def _attn_merge_kernel(n1_ref, n4_ref, n16_ref, s1_ref, s4_ref, s16_ref, ex_ref, nw_ref, y_ref,
                       t4_ref, t16_ref, out_ref):
    tm = y_ref.shape[0]
    lane = lax.broadcasted_iota(jnp.int32, (1, LANES), 1)
    head = lane < ATT_HEADS

    def natural(ref, tmp_ref, s):
        d = ref.shape[1]
        if d == 1:
            return ref[0, 0, :, s * LANES:(s + 1) * LANES]
        for r in range(d):
            tmp_ref[pl.ds(r, tm // d, stride=d), :] = ref[0, r, :, s * LANES:(s + 1) * LANES]
        return tmp_ref[...]

    st = [natural(s1_ref, None, 0), natural(s4_ref, t4_ref, 0), natural(s16_ref, t16_ref, 0)]
    m = jnp.maximum(jnp.maximum(st[0], st[1]), st[2])
    scale = [jnp.exp2(x - m) for x in st]
    den = sum(sc * pltpu.roll(x, LANES - ATT_HEADS, axis=1) for sc, x in zip(scale, st))
    inv = 1.0 / den
    ex = ex_ref[...]

    def spread(w):
        w = jnp.where(head, w, 0.0)
        hi = w.astype(BF16)
        lo = (w - hi.astype(F32)).astype(BF16)
        return (jnp.dot(hi, ex, preferred_element_type=F32) + jnp.dot(lo, ex, preferred_element_type=F32))

    wide = [spread(sc * inv) for sc in scale]
    refs = ((n1_ref, None), (n4_ref, t4_ref), (n16_ref, t16_ref))
    for s in range(ATT_WIDTH // LANES):
        sl = slice(s * LANES, (s + 1) * LANES)
        out_ref[:, sl] = sum(w[:, sl] * natural(ref, tmp, s) for w, (ref, tmp) in zip(wide, refs))
    y_ref[...] = _rms(out_ref[...], nw_ref[...]).astype(BF16)


def _attn_merge(nums, stats, norm_w, *, tm=512):
    b, _, s, _ = nums[0].shape
    n, tpb = b * s, s // tm
    expand = np.zeros((LANES, ATT_WIDTH), np.float32)
    for h in range(ATT_HEADS):
        expand[h, h * ATT_HEAD_DIM:(h + 1) * ATT_HEAD_DIM] = 1.0
    est = (2 * 3 * (_nbytes((tm, ATT_WIDTH), F32) + _nbytes((tm, LANES), F32)) + 5 * _nbytes((tm, ATT_WIDTH), F32)
           + 2 * _nbytes((LANES, ATT_WIDTH), F32))

    def blk(d, c):
        return pl.BlockSpec((1, d, tm // d, c), lambda i: (i // tpb, 0, i % tpb, 0))

    fixed = lambda i: (0, 0)
    return pl.pallas_call(
        _attn_merge_kernel,
        grid=(n // tm,),
        in_specs=[blk(d, ATT_WIDTH) for d in DILATIONS] + [blk(d, LANES) for d in DILATIONS] + [
            pl.BlockSpec((LANES, ATT_WIDTH), fixed), pl.BlockSpec((1, ATT_WIDTH), fixed)],
        out_specs=pl.BlockSpec((tm, ATT_WIDTH), lambda i: (i, 0)),
        out_shape=jax.ShapeDtypeStruct((n, ATT_WIDTH), BF16),
        scratch_shapes=[pltpu.VMEM((tm, LANES), F32), pltpu.VMEM((tm, LANES), F32),
                        pltpu.VMEM((tm, ATT_WIDTH), F32)],
        compiler_params=_params(("parallel",), est),
        name="attn_merge",
    )(*nums, *stats, jnp.asarray(expand, dtype=BF16), norm_w.reshape(1, ATT_WIDTH))


CONV_ROWS = 512
CONV_COLS = 256


def _conv_silu_kernel(x_ref, w_ref, b_ref, y_ref, *, seq):
    cb = pl.program_id(1)
    scale = jnp.where(cb >= MLSTM_WIDTH // CONV_COLS, MLSTM_HEAD_DIM ** -0.5, 1.0).astype(F32)
    w0, w1, w2 = w_ref[0:1, :], w_ref[1:2, :], w_ref[2:3, :]
    bias = b_ref[...]
    rows = lax.broadcasted_iota(jnp.int32, (CONV_ROWS, 1), 0)

    def body(c, carry):
        c0 = pl.multiple_of(c * CONV_ROWS, CONV_ROWS)
        xc = x_ref[0, pl.ds(c0, CONV_ROWS), :]
        before = x_ref[0, pl.ds(pl.multiple_of(jnp.maximum(c0 - 8, 0), 8), 8), :][7:8, :]
        after = x_ref[0, pl.ds(pl.multiple_of(jnp.minimum(c0 + CONV_ROWS, seq - 8), 8), 8), :][0:1, :]
        before = jnp.where(c0 > 0, before, 0.0)
        after = jnp.where(c0 + CONV_ROWS < seq, after, 0.0)
        up = jnp.where(rows == 0, before, pltpu.roll(xc, 1, axis=0))
        dn = jnp.where(rows == CONV_ROWS - 1, after, pltpu.roll(xc, CONV_ROWS - 1, axis=0))
        y = w0 * up + w1 * xc + w2 * dn + bias
        y_ref[0, pl.ds(c0, CONV_ROWS), :] = (y * jax.nn.sigmoid(y) * scale).astype(BF16)
        return carry

    lax.fori_loop(0, seq // CONV_ROWS, body, 0)


def _conv_silu(x, conv_w, conv_b):
    b, s, c = x.shape
    est = 2 * _nbytes((s, CONV_COLS), F32) + 2 * _nbytes((s, CONV_COLS), BF16) + 8 * _nbytes((CONV_ROWS, CONV_COLS), F32)
    return pl.pallas_call(
        functools.partial(_conv_silu_kernel, seq=s),
        grid=(b, c // CONV_COLS),
        in_specs=[
            pl.BlockSpec((1, s, CONV_COLS), lambda bi, cb: (bi, 0, cb)),
            pl.BlockSpec((3, CONV_COLS), lambda bi, cb: (0, cb)),
            pl.BlockSpec((1, CONV_COLS), lambda bi, cb: (0, cb)),
        ],
        out_specs=pl.BlockSpec((1, s, CONV_COLS), lambda bi, cb: (bi, 0, cb)),
        out_shape=jax.ShapeDtypeStruct((b, s, c), BF16),
        compiler_params=_params(("parallel", "parallel"), est),
        name="conv_silu",
    )(x, conv_w, conv_b.reshape(1, c))


EXT = MLSTM_HEAD_DIM + LANES


def _log_sigmoid(x):
    return jnp.minimum(x, 0.0) - jnp.log1p(jnp.exp(-jnp.abs(x)))


def _mlstm_kernel(*refs, reverse, tile):
    if reverse:
        q_ref, k_ref, v_ref, gt_ref, hf_ref, o_ref, nw_ref, out_ref, cext_ref, m_ref, hb_ref = refs
    else:
        q_ref, k_ref, v_ref, gt_ref, out_ref, cext_ref, m_ref = refs
    C = MLSTM_CHUNK[reverse]
    nchunks = tile // C

    @pl.when(pl.program_id(1) == 0)
    def _():
        cext_ref[...] = jnp.zeros_like(cext_ref)
        m_ref[...] = jnp.full_like(m_ref, NEG)

    r_i = lax.broadcasted_iota(jnp.int32, (C, C), 0)
    c_i = lax.broadcasted_iota(jnp.int32, (C, C), 1)
    if reverse:
        causal, causal_t = c_i >= r_i, r_i >= c_i
    else:
        causal, causal_t = c_i <= r_i, r_i <= c_i
    tri_row = jnp.where(causal_t, 1.0, 0.0).astype(F32)
    tri_col = jnp.where(causal, 1.0, 0.0).astype(F32)
    eye = jnp.where(r_i == c_i, 1.0, 0.0).astype(F32)
    ones_lane = jnp.ones((C, LANES), BF16)
    hi = lax.Precision.HIGHEST
    g_i = 2 * MLSTM_HEADS if reverse else 0
    g_f = g_i + MLSTM_HEADS

    def chunk(ci):
        c = (nchunks - 1 - ci) if reverse else ci
        t0 = pl.multiple_of(c * C, C)
        gates = gt_ref[:, pl.ds(t0, C)]
        lsig = _log_sigmoid(gates)
        cum_rows = jnp.dot(lsig, tri_row, preferred_element_type=F32, precision=hi)
        cum_cols = lax.dot_general(tri_col, lsig, NT_DIMS, preferred_element_type=F32, precision=hi)
        gate_cols = lax.dot_general(eye, gates, NT_DIMS, preferred_element_type=F32, precision=hi)
        li, li_cols = gates[g_i:g_i + MLSTM_HEADS], gate_cols[:, g_i:g_i + MLSTM_HEADS]
        b_rows, b_cols = cum_rows[g_f:g_f + MLSTM_HEADS], cum_cols[:, g_f:g_f + MLSTM_HEADS]
        for hd in range(MLSTM_HEADS):
            hs = slice(hd * MLSTM_HEAD_DIM, (hd + 1) * MLSTM_HEAD_DIM)
            q = q_ref[0, pl.ds(t0, C), hs]
            k = k_ref[0, pl.ds(t0, C), hs]
            v = v_ref[0, pl.ds(t0, C), hs]
            vext = jnp.concatenate([v, ones_lane], axis=1)
            bc, br = b_cols[:, hd:hd + 1], b_rows[hd:hd + 1, :]
            lir, lic = li[hd:hd + 1, :], li_cols[:, hd:hd + 1]
            m_prev = m_ref[hd, 0:1, 0:1]
            dm = jnp.where(causal, bc - br + lir, -jnp.inf)
            inter = bc + m_prev
            m_t = jnp.maximum(inter, jnp.max(dm, axis=1, keepdims=True))
            sqk = lax.dot_general(q, k, NT_DIMS, preferred_element_type=F32)
            w_intra = (jnp.exp(dm - m_t) * sqk).astype(BF16)
            w_inter = jnp.exp(inter - m_t)
            cext = cext_ref[hd]
            num = (jnp.dot(w_intra, vext, preferred_element_type=F32)
                   + w_inter * jnp.dot(q, cext.astype(BF16), preferred_element_type=F32))
            den =jnp.concatenate([num[:, MLSTM_HEAD_DIM:]] * (MLSTM_HEAD_DIM // LANES), axis=1)
            h = num[:, :MLSTM_HEAD_DIM] / jnp.maximum(jnp.abs(den), jnp.exp(-m_t))
            g = br[:, 0:1] if reverse else br[:, C - 1:C]
            m_new = jnp.maximum(g + m_prev, jnp.max(g - br + lir, axis=1, keepdims=True))
            w_a = jnp.exp(g - bc + lic - m_new)
            decay = jnp.exp(g + m_prev - m_new)
            vw = (vext.astype(F32) * w_a).astype(BF16)
            cext_ref[hd] = decay * cext + lax.dot_general(k, vw, TN_DIMS, preferred_element_type=F32)
            m_ref[hd] = jnp.broadcast_to(m_new, m_ref.shape[1:])
            if reverse:
                hb_ref[pl.ds(t0, C), hs] = h
            else:
                out_ref[0, pl.ds(t0, C), hs] = h

    def body(i, carry):
        for u in range(MLSTM_UNROLL):
            chunk(i * MLSTM_UNROLL + u)
        return carry

    lax.fori_loop(0, nchunks // MLSTM_UNROLL, body, 0)

    if reverse:
        def norm(i, carry):
            rows = pl.ds(pl.multiple_of(i * NORM_ROWS, NORM_ROWS), NORM_ROWS)
            for hd in range(MLSTM_HEADS):
                hs = slice(hd * MLSTM_HEAD_DIM, (hd + 1) * MLSTM_HEAD_DIM)
                hsum = hb_ref[rows, hs] + hf_ref[0, rows, hs]
                mu = jnp.mean(hsum, axis=1, keepdims=True)
                dev = hsum - mu
                var = jnp.mean(dev * dev, axis=1, keepdims=True)
                y = dev * lax.rsqrt(var + EPS) * nw_ref[:, hs] * jax.nn.sigmoid(o_ref[0, rows, hs])
                out_ref[0, rows, hs] = y.astype(BF16)
            return carry

        lax.fori_loop(0, tile // NORM_ROWS, norm, 0)


def _mlstm(qk, v, gates_t, o_pre, norm_w, *, tile=512):
    b, s, _ = v.shape
    ns = s // tile
    w = MLSTM_WIDTH
    scratch = [pltpu.VMEM((MLSTM_HEADS, MLSTM_HEAD_DIM, EXT), F32), pltpu.VMEM((MLSTM_HEADS, 8, LANES), F32)]
    est = (2 * 3 * _nbytes((tile, w), BF16) + 2 * 3 * _nbytes((tile, w), F32)
           + _nbytes((MLSTM_HEADS, MLSTM_HEAD_DIM, EXT), F32) + 16 * _nbytes((max(MLSTM_CHUNK.values()), EXT), F32))

    def specs(order):
        return [
            pl.BlockSpec((1, tile, w), lambda bi, si: (bi, order(si), 0)),
            pl.BlockSpec((1, tile, w), lambda bi, si: (bi, order(si), 1)),
            pl.BlockSpec((1, tile, w), lambda bi, si: (bi, order(si), 0)),
            pl.BlockSpec((N_GATES, tile), lambda bi, si: (0, bi * ns + order(si))),
        ]

    fwd = lambda si: si
    bwd = lambda si: ns - 1 - si
    h_fwd = pl.pallas_call(
        functools.partial(_mlstm_kernel, reverse=False, tile=tile),
        grid=(b, ns),
        in_specs=specs(fwd),
        out_specs=pl.BlockSpec((1, tile, w), lambda bi, si: (bi, si, 0)),
        out_shape=jax.ShapeDtypeStruct((b, s, w), F32),
        scratch_shapes=scratch,
        compiler_params=_params(("parallel", "arbitrary"), est),
        name="mlstm_fwd",
    )(qk, qk, v, gates_t)
    return pl.pallas_call(
        functools.partial(_mlstm_kernel, reverse=True, tile=tile),
        grid=(b, ns),
        in_specs=specs(bwd) + [
            pl.BlockSpec((1, tile, w), lambda bi, si: (bi, bwd(si), 0)),
            pl.BlockSpec((1, tile, w), lambda bi, si: (bi, bwd(si), 0)),
            pl.BlockSpec((1, w), lambda bi, si: (0, 0)),
        ],
        out_specs=pl.BlockSpec((1, tile, w), lambda bi, si: (bi, bwd(si), 0)),
        out_shape=jax.ShapeDtypeStruct((b, s, w), BF16),
        scratch_shapes=scratch + [pltpu.VMEM((tile, w), F32)],
        compiler_params=_params(("parallel", "arbitrary"), est + _nbytes((tile, w), F32)),
        name="mlstm_bwd",
    )(qk, qk, v, gates_t, h_fwd, o_pre, norm_w.reshape(1, w))


def _out_proj_kernel(x_ref, ya_ref, ym_ref, wa_ref, wm_ref, o_ref):
    o_ref[...] = (x_ref[...]
                  + jnp.dot(ya_ref[...], wa_ref[...], preferred_element_type=F32)
                  + jnp.dot(ym_ref[...], wm_ref[...], preferred_element_type=F32))


def _out_proj(x, y_att, y_mem, w_out, *, tm=512):
    n, d = x.shape
    est = (2 * 2 * _nbytes((tm, d), F32) + 2 * 2 * _nbytes((tm, ATT_WIDTH), BF16)
           + 2 * _nbytes((d, d), BF16) + 2 * _nbytes((tm, d), F32))
    row = lambda i: (i, 0)
    return pl.pallas_call(
        _out_proj_kernel,
        grid=(n // tm,),
        in_specs=[
            pl.BlockSpec((tm, d), row),
            pl.BlockSpec((tm, ATT_WIDTH), row),
            pl.BlockSpec((tm, MLSTM_WIDTH), row),
            pl.BlockSpec((ATT_WIDTH, d), lambda i: (0, 0)),
            pl.BlockSpec((MLSTM_WIDTH, d), lambda i: (1, 0)),
        ],
        out_specs=pl.BlockSpec((tm, d), row),
        out_shape=jax.ShapeDtypeStruct((n, d), F32),
        compiler_params=_params(("parallel",), est),
        name="out_proj",
    )(x, y_att, y_mem, w_out, w_out)


def _ple_final_kernel(x_ref, p_ref, nw_ref, wg_ref, wp_ref, fw_ref, o_ref):
    x = x_ref[...]
    h = _rms(x, nw_ref[...]).astype(BF16)
    gate = jax.nn.sigmoid(jnp.dot(h, wg_ref[...], preferred_element_type=F32))
    proj = jnp.dot(p_ref[...].astype(BF16), wp_ref[...], preferred_element_type=F32)
    o_ref[...] = _rms(x + gate * proj, fw_ref[...])


def _ple_final(x, p, norm_w, w_gate, w_proj, final_w, *, tm=512):
    n, d = x.shape
    est = (2 * 2 * _nbytes((tm, d), F32) + 2 * _nbytes((tm, PLE_DIM), F32) + 2 * _nbytes((d, d), BF16)
           + 2 * _nbytes((PLE_DIM, d), BF16) + 4 * _nbytes((tm, d), F32))
    row = lambda i: (i, 0)
    fixed = lambda i: (0, 0)
    return pl.pallas_call(
        _ple_final_kernel,
        grid=(n // tm,),
        in_specs=[
            pl.BlockSpec((tm, d), row),
            pl.BlockSpec((tm, PLE_DIM), row),
            pl.BlockSpec((1, d), fixed),
            pl.BlockSpec((d, d), fixed),
            pl.BlockSpec((PLE_DIM, d), fixed),
            pl.BlockSpec((1, d), fixed),
        ],
        out_specs=pl.BlockSpec((tm, d), row),
        out_shape=jax.ShapeDtypeStruct((n, d), F32),
        compiler_params=_params(("parallel",), est),
        name="ple_final",
    )(x, p, norm_w.reshape(1, d), w_gate, w_proj, final_w.reshape(1, d))


def _trunk(x, p, wts):
    b, s, d = x.shape
    n = b * s
    h = _ffn(x.reshape(n, d), wts["ffn1_norm"], wts["ffn1_w_in"], wts["ffn1_w_out"])
    *atts, mqk, mv, mo, gates_t = _in_proj(h, wts["mix_norm"], wts["w_in_main"], wts["w_in_gates_t"],
                                           wts["b_gates"], seq=s)
    nums, stats = zip(*[_attn_branch(att, wts["bias"][d_]) for att, d_ in zip(atts, DILATIONS)])
    y_att = _attn_merge(nums, stats, wts["attn_out_norm"])
    qk = _conv_silu(mqk.reshape(b, s, 2 * MLSTM_WIDTH), wts["conv_w"], wts["conv_b"])
    y_mem = _mlstm(qk, mv.reshape(b, s, MLSTM_WIDTH), gates_t, mo.reshape(b, s, MLSTM_WIDTH),
                   wts["mlstm_out_norm"])
    h = _out_proj(h, y_att, y_mem.reshape(n, MLSTM_WIDTH), wts["w_out"])
    h = _ffn(h, wts["ffn2_norm"], wts["ffn2_w_in"], wts["ffn2_w_out"])
    y = _ple_final(h, p.reshape(n, PLE_DIM), wts["ple_norm"], wts["ple_w_gate"], wts["ple_w_proj"],
                   wts["final_norm"])
    return y.reshape(b, s, d)


def kernel(x_prompt, x_sample, p_prompt, p_sample, rel_table, ffn1_norm, ffn1_w_in, ffn1_w_out, mix_norm, w_in, b_gates, conv_w, conv_b, attn_out_norm, mlstm_out_norm, w_out, ffn2_norm, ffn2_w_in, ffn2_w_out, ple_norm, ple_w_gate, ple_w_proj, final_norm):
    depth = ffn1_norm.shape[0]
    assert depth == 1
    i = 0
    n_main = 3 * ATT_WIDTH + 4 * MLSTM_WIDTH
    col_scale = jnp.concatenate([jnp.full((ATT_WIDTH,), ATT_HEAD_DIM ** -0.5 * LOG2_E, F32),
                                 jnp.ones((n_main - ATT_WIDTH,), F32)])
    wts = dict(
        ffn1_norm=ffn1_norm[i], ffn1_w_in=ffn1_w_in[i].astype(BF16), ffn1_w_out=ffn1_w_out[i].astype(BF16),
        mix_norm=mix_norm[i],
        w_in_main=(w_in[i][:, :n_main] * col_scale).astype(BF16),
        w_in_gates_t=w_in[i][:, n_main:].T.astype(BF16),
        b_gates=b_gates[i], conv_w=conv_w[i], conv_b=conv_b[i],
        attn_out_norm=attn_out_norm[i], mlstm_out_norm=mlstm_out_norm[i],
        w_out=w_out[i].astype(BF16),
        ffn2_norm=ffn2_norm[i], ffn2_w_in=ffn2_w_in[i].astype(BF16), ffn2_w_out=ffn2_w_out[i].astype(BF16),
        ple_norm=ple_norm[i], ple_w_gate=ple_w_gate[i].astype(BF16), ple_w_proj=ple_w_proj[i].astype(BF16),
        final_norm=final_norm,
        bias={d_: _branch_bias(rel_table, d_) for d_ in DILATIONS},
    )
    return (_trunk(x_prompt, p_prompt[i], wts), _trunk(x_sample, p_sample[i], wts))
```
